```python
import jax, jax.numpy as jnp
from jax import lax
import numpy as np

D_MODEL = 2048
BATCH = 2
SEQ = 8192
DEPTH = 2

D_MIX = D_MODEL
CONV_W = D_MIX // 4
CONV_K = 3
SB_HEADS = 8
SB_HD = 128
SB_W = SB_HEADS * SB_HD
SB_BLOCK = 128
GLA_HEADS = 4
GLA_W = D_MIX - CONV_W - SB_W
GLA_DV = GLA_W // GLA_HEADS
GLA_DK = GLA_DV // 2
GLA_LR = 16
GLA_TAU = 16.0
GLA_CHUNK = 64
EPS = 1e-6
IN_SPLITS = (CONV_W,) * 4 + (SB_W,) * 4 + (GLA_HEADS * GLA_DK,) * 2 + (GLA_W,) * 2 + (GLA_LR,)
N_IN = sum(IN_SPLITS)

kernel_name = "hybrid_conv_stickbreak_gla_parallel"


def _rmsnorm(x, g):
    xf = x.astype(jnp.float32)
    y = xf * lax.rsqrt(jnp.mean(xf * xf, axis=-1, keepdims=True) + EPS)
    return y * g.astype(jnp.float32)


def _short_conv(h, b_gate, c_gate, w, bias):
    u = c_gate * h
    y = lax.conv_general_dilated(
        u, w[:, None, :], window_strides=(1,), padding=[(CONV_K - 1, 0)],
        dimension_numbers=("NWC", "WIO", "NWC"), feature_group_count=CONV_W)
    return b_gate.astype(jnp.float32) * (y + bias).astype(jnp.float32)


def _stick_breaking(q, k, v):
    b, s, nh, hd = q.shape
    qh = q.transpose(0, 2, 1, 3) * (hd ** -0.5)
    kh = k.transpose(0, 2, 1, 3)
    vh = v.transpose(0, 2, 1, 3)
    kpos = jnp.arange(s)

    def block(i):
        start = i * SB_BLOCK
        qb = lax.dynamic_slice_in_dim(qh, start, SB_BLOCK, axis=2)
        z = jnp.einsum("bhqd,bhkd->bhqk", qb, kh)
        qpos = start + jnp.arange(SB_BLOCK)
        mask = kpos[None, :] < qpos[:, None]
        log_keep = jnp.where(mask, jax.nn.log_sigmoid(-z), 0.0)
        later = lax.cumsum(log_keep, axis=3, reverse=True) - log_keep
        w = jnp.where(mask, jnp.exp(jax.nn.log_sigmoid(z) + later), 0.0)
        return jnp.einsum("bhqk,bhkd->bhqd", w, vh)

    out = lax.map(block, jnp.arange(s // SB_BLOCK))
    return out.transpose(1, 0, 3, 2, 4).reshape(b, s, nh * hd)


def _gla(q, k, v, g):
    b, s, nh, dk = q.shape
    dv = v.shape[-1]
    nc = s // GLA_CHUNK
    q = q * (dk ** -0.5)

    def to_chunks(t):
        return t.reshape(b, nc, GLA_CHUNK, nh, t.shape[-1]).transpose(1, 0, 3, 2, 4)

    causal = jnp.tril(jnp.ones((GLA_CHUNK, GLA_CHUNK), dtype=bool))[:, :, None]

    def step(state, inp):
        qc, kc, vc, gc = inp
        cum = jnp.cumsum(gc, axis=2)
        o_inter = jnp.einsum("bhck,bhkv->bhcv", qc * jnp.exp(cum), state)
        diff = cum[:, :, :, None, :] - cum[:, :, None, :, :]
        decay = jnp.where(causal, jnp.exp(jnp.minimum(diff, 0.0)), 0.0)
        att = jnp.einsum("bhtk,bhsk,bhtsk->bhts", qc, kc, decay)
        o_intra = jnp.einsum("bhts,bhsv->bhtv", att, vc)
        last = cum[:, :, -1:, :]
        k_dec = kc * jnp.exp(last - cum)
        new_state = state * jnp.exp(last[:, :, 0, :, None]) + jnp.einsum("bhsk,bhsv->bhkv", k_dec, vc)
        return new_state, o_inter + o_intra

    init = jnp.zeros((b, nh, dk, dv), jnp.float32)
    _, o = lax.scan(step, init, (to_chunks(q), to_chunks(k), to_chunks(v), to_chunks(g)))
    return o.transpose(1, 0, 3, 2, 4).reshape(b, s, nh, dv)


def _layer(x, norm_g, w_in, conv_w, conv_b, sb_qn_g, sb_kn_g, gla_w_up, gla_b_up, gla_on_g, w_out):
    b, s, _ = x.shape
    h = _rmsnorm(x, norm_g).astype(x.dtype)
    proj = h @ w_in
    split_points = np.cumsum(IN_SPLITS)[:-1].tolist()
    (cv_h, cv_b, cv_c, cv_g,
     sb_q, sb_k, sb_v, sb_g,
     gl_q, gl_k, gl_v, gl_g, gl_lr) = jnp.split(proj, split_points, axis=-1)
    f32 = jnp.float32

    y_conv = _short_conv(cv_h, cv_b, cv_c, conv_w, conv_b) * jax.nn.silu(cv_g.astype(f32))

    q = _rmsnorm(sb_q.reshape(b, s, SB_HEADS, SB_HD), sb_qn_g)
    k = _rmsnorm(sb_k.reshape(b, s, SB_HEADS, SB_HD), sb_kn_g)
    v = sb_v.reshape(b, s, SB_HEADS, SB_HD).astype(f32)
    y_sb = _stick_breaking(q, k, v) * jax.nn.silu(sb_g.astype(f32))

    g = jax.nn.log_sigmoid((gl_lr @ gla_w_up + gla_b_up).astype(f32)) / GLA_TAU
    o = _gla(gl_q.reshape(b, s, GLA_HEADS, GLA_DK).astype(f32),
             gl_k.reshape(b, s, GLA_HEADS, GLA_DK).astype(f32),
             gl_v.reshape(b, s, GLA_HEADS, GLA_DV).astype(f32),
             g.reshape(b, s, GLA_HEADS, GLA_DK))
    y_gla = _rmsnorm(o, gla_on_g).reshape(b, s, GLA_W) * jax.nn.silu(gl_g.astype(f32))

    mix = jnp.concatenate([y_conv, y_sb, y_gla], axis=-1).astype(x.dtype)
    return x + mix @ w_out


def setup_inputs(seed: int = 0) -> dict:
    key = jax.random.key(seed)
    ks = jax.random.split(key, 12)
    f32 = jnp.float32
    n = lambda k, shape: jax.random.normal(k, shape, f32)
    x = n(ks[0], (BATCH, SEQ, D_MODEL))
    norm_g = 1.0 + 0.02 * n(ks[1], (DEPTH, D_MODEL))
    w_in = n(ks[2], (DEPTH, D_MODEL, N_IN)) * D_MODEL ** -0.5
    conv_w = n(ks[3], (DEPTH, CONV_K, CONV_W)) * CONV_K ** -0.5
    conv_b = 0.02 * n(ks[4], (DEPTH, CONV_W))
    sb_qn_g = 1.0 + 0.02 * n(ks[5], (DEPTH, SB_HD))
    sb_kn_g = 1.0 + 0.02 * n(ks[6], (DEPTH, SB_HD))
    gla_w_up = n(ks[7], (DEPTH, GLA_LR, GLA_HEADS * GLA_DK)) * GLA_LR ** -0.5
    gla_b_up = 0.02 * n(ks[8], (DEPTH, GLA_HEADS * GLA_DK))
    gla_on_g = 1.0 + 0.02 * n(ks[9], (DEPTH, GLA_DV))
    w_out = n(ks[10], (DEPTH, D_MIX, D_MODEL)) * (D_MIX ** -0.5) * 0.5
    return {"x": x, "norm_g": norm_g, "w_in": w_in, "conv_w": conv_w, "conv_b": conv_b,
            "sb_qn_g": sb_qn_g, "sb_kn_g": sb_kn_g, "gla_w_up": gla_w_up, "gla_b_up": gla_b_up,
            "gla_on_g": gla_on_g, "w_out": w_out}


def reference(x, norm_g, w_in, conv_w, conv_b, sb_qn_g, sb_kn_g, gla_w_up, gla_b_up, gla_on_g, w_out):
    h = x
    for l in range(DEPTH):
        h = _layer(h, norm_g[l], w_in[l], conv_w[l], conv_b[l], sb_qn_g[l], sb_kn_g[l],
                   gla_w_up[l], gla_b_up[l], gla_on_g[l], w_out[l])
    return h
```

```python
import functools

import numpy as np
import jax
import jax.numpy as jnp
from jax import lax
from jax.experimental import pallas as pl
from jax.experimental.pallas import tpu as pltpu

SB_HEADS = 8
SB_HD = 128
SB_W = SB_HEADS * SB_HD
GLA_HEADS = 4
GLA_DV = 128
GLA_DK = 64
GLA_W = GLA_HEADS * GLA_DV
GLA_KW = GLA_HEADS * GLA_DK
GLA_LR = 16
GLA_TAU = 16.0
CONV_K = 3
EPS = 1e-6

LANES = 128
SUBLANES = 8
VMEM_LIMIT_BYTES = 56 * 1024 * 1024

ROW_TILE = 512
COL_CHUNK = 512
GLA_CHUNK = 128
GLA_LEVELS = 7
SB_BLOCK = 256

F32 = jnp.float32
BF16 = jnp.bfloat16


def _dot(a, b):
    return jnp.dot(a, b, preferred_element_type=F32)


def _dot_nt(a, b):
    return lax.dot_general(a, b, (((1,), (1,)), ((), ())), preferred_element_type=F32)


def _silu(a):
    return a / (1.0 + jnp.exp(-a))


def _softplus(z):
    return jnp.maximum(z, 0.0) + jnp.log(1.0 + jnp.exp(-jnp.abs(z)))


def _normed_input(x_ref, ng_ref):
    xf = x_ref[...]
    ms = jnp.mean(xf * xf, axis=-1, keepdims=True)
    return (xf * lax.rsqrt(ms + EPS) * ng_ref[...]).astype(BF16)


def _head_rmsnorm(a, gain):
    outs = []
    for j in range(a.shape[1] // LANES):
        aj = a[:, j * LANES:(j + 1) * LANES]
        ms = jnp.mean(aj * aj, axis=-1, keepdims=True)
        outs.append(aj * lax.rsqrt(ms + EPS) * gain)
    return outs


def _attn_proj_kernel(x_ref, ng_ref, w_ref, qg_ref, kg_ref, q_ref, k_ref, v_ref, g_ref):
    h = _normed_input(x_ref, ng_ref)
    q_scale = SB_HD ** -0.5
    n_chunks = SB_W // COL_CHUNK
    for c in range(n_chunks):
        cols = slice(c * COL_CHUNK, (c + 1) * COL_CHUNK)
        qa = _dot(h, w_ref[:, c * COL_CHUNK:(c + 1) * COL_CHUNK])
        for j, qn in enumerate(_head_rmsnorm(qa, qg_ref[...])):
            q_ref[:, c * COL_CHUNK + j * LANES:c * COL_CHUNK + (j + 1) * LANES] = (qn * q_scale).astype(BF16)
        ka = _dot(h, w_ref[:, SB_W + c * COL_CHUNK:SB_W + (c + 1) * COL_CHUNK])
        for j, kn in enumerate(_head_rmsnorm(ka, kg_ref[...])):
            k_ref[:, c * COL_CHUNK + j * LANES:c * COL_CHUNK + (j + 1) * LANES] = kn.astype(BF16)
        va = _dot(h, w_ref[:, 2 * SB_W + c * COL_CHUNK:2 * SB_W + (c + 1) * COL_CHUNK])
        v_ref[:, cols] = va.astype(BF16)
        ga = _dot(h, w_ref[:, 3 * SB_W + c * COL_CHUNK:3 * SB_W + (c + 1) * COL_CHUNK])
        g_ref[:, cols] = _silu(ga)


def _attn_proj(x2, norm_g, w_b, qn_g, kn_g):
    m, d = x2.shape
    row = lambda i: (i, 0)
    const = lambda i: (0, 0)
    return pl.pallas_call(
        _attn_proj_kernel,
        grid=(m // ROW_TILE,),
        in_specs=[
            pl.BlockSpec((ROW_TILE, d), row),
            pl.BlockSpec((1, d), const),
            pl.BlockSpec((d, 4 * SB_W), const, pipeline_mode=pl.Buffered(1)),
            pl.BlockSpec((1, SB_HD), const),
            pl.BlockSpec((1, SB_HD), const),
        ],
        out_specs=[pl.BlockSpec((ROW_TILE, SB_W), row)] * 4,
        out_shape=[
            jax.ShapeDtypeStruct((m, SB_W), BF16),
            jax.ShapeDtypeStruct((m, SB_W), BF16),
            jax.ShapeDtypeStruct((m, SB_W), BF16),
            jax.ShapeDtypeStruct((m, SB_W), F32),
        ],
        compiler_params=pltpu.CompilerParams(
            dimension_semantics=("parallel",), vmem_limit_bytes=VMEM_LIMIT_BYTES),
        name="attn_proj",
    )(x2, norm_g, w_b, qn_g, kn_g)


def _conv_gla_proj_kernel(x_ref, ng_ref, w_ref, cw_ref, cb_ref, wup_ref, bup_ref,
                          mixc_ref, glq_ref, glk_ref, glv_ref, gate_ref, gdec_ref, ubuf_ref,
                          *, conv_w):
    j = pl.program_id(1)
    tm = x_ref.shape[0]
    h = _normed_input(x_ref, ng_ref)

    cvh = _dot(h, w_ref[:, 0:conv_w])
    cvb = _dot(h, w_ref[:, conv_w:2 * conv_w])
    cvc = _dot(h, w_ref[:, 2 * conv_w:3 * conv_w])
    cvg = _dot(h, w_ref[:, 3 * conv_w:4 * conv_w])
    u = cvc * cvh
    halo = SUBLANES

    @pl.when(j == 0)
    def _():
        ubuf_ref[0:halo, :] = jnp.zeros((halo, conv_w), F32)

    ubuf_ref[halo:halo + tm, :] = u
    um1 = ubuf_ref[halo - 1:halo - 1 + tm, :]
    um2 = ubuf_ref[halo - 2:halo - 2 + tm, :]
    y = cw_ref[0:1, :] * um2 + cw_ref[1:2, :] * um1 + cw_ref[2:3, :] * u + cb_ref[...]
    mixc_ref[...] = (cvb * y * _silu(cvg)).astype(BF16)
    ubuf_ref[0:halo, :] = u[tm - halo:tm, :]

    base = 4 * conv_w
    glq_ref[...] = _dot(h, w_ref[:, base:base + GLA_KW])
    glk_ref[...] = _dot(h, w_ref[:, base + GLA_KW:base + 2 * GLA_KW])
    base += 2 * GLA_KW
    glv_ref[...] = _dot(h, w_ref[:, base:base + GLA_W]).astype(BF16)
    gate_ref[...] = _silu(_dot(h, w_ref[:, base + GLA_W:base + 2 * GLA_W]))
    base += 2 * GLA_W
    lr = _dot(h, w_ref[:, base:base + LANES])
    pre = _dot(lr.astype(BF16), wup_ref[...]) + bup_ref[...]
    log_sig = jnp.minimum(pre, 0.0) - jnp.log(1.0 + jnp.exp(-jnp.abs(pre)))
    gdec_ref[...] = log_sig / GLA_TAU


def _conv_gla_proj(x2, norm_g, w_a, conv_w, conv_b, wup, bup, *, batch, seq):
    m, d = x2.shape
    cw = conv_w.shape[1]
    n_tiles = seq // ROW_TILE
    row = lambda b, j: (b * n_tiles + j, 0)
    const = lambda b, j: (0, 0)
    ncols = w_a.shape[1]
    widths = [(cw, BF16), (GLA_KW, F32), (GLA_KW, F32), (GLA_W, BF16), (GLA_W, F32), (GLA_KW, F32)]
    return pl.pallas_call(
        functools.partial(_conv_gla_proj_kernel, conv_w=cw),
        grid=(batch, n_tiles),
        in_specs=[
            pl.BlockSpec((ROW_TILE, d), row),
            pl.BlockSpec((1, d), const),
            pl.BlockSpec((d, ncols), const, pipeline_mode=pl.Buffered(1)),
            pl.BlockSpec((CONV_K, cw), const),
            pl.BlockSpec((1, cw), const),
            pl.BlockSpec((LANES, GLA_KW), const),
            pl.BlockSpec((1, GLA_KW), const),
        ],
        out_specs=[pl.BlockSpec((ROW_TILE, w), row) for w, _ in widths],
        out_shape=[jax.ShapeDtypeStruct((m, w), dt) for w, dt in widths],
        scratch_shapes=[pltpu.VMEM((ROW_TILE + SUBLANES, cw), F32)],
        compiler_params=pltpu.CompilerParams(
            dimension_semantics=("arbitrary", "arbitrary"), vmem_limit_bytes=VMEM_LIMIT_BYTES),
        name="conv_gla_proj",
    )(x2, norm_g, w_a, conv_w, conv_b, wup, bup)


def _gla_tables():
    c, nl = GLA_CHUNK, GLA_LEVELS
    t = np.arange(c)
    sel = np.zeros((nl + 1, c, c), np.float32)
    msk = np.zeros((nl + 1, c, c), np.float32)
    for l in range(nl):
        up = ((t >> l) & 1) == 1
        lo_start = (t >> l) << l
        nxt = ((t >> l) + 1) << l
        for r in range(c):
            if up[r]:
                sel[l, r, lo_start[r]:r + 1] = 1.0
            else:
                sel[l, r, r + 1:nxt[r]] = 1.0
        same = (t[:, None] >> (l + 1)) == (t[None, :] >> (l + 1))
        msk[l] = (up[:, None] & ~up[None, :] & same).astype(np.float32)
    sel[nl] = (t[None, :] <= t[:, None]).astype(np.float32)
    msk[nl] = np.eye(c, dtype=np.float32)
    sel = sel.reshape((nl + 1) * c, c)
    msk = np.tile(msk, (1, 1, GLA_HEADS))
    head_of_k = np.arange(GLA_KW) // GLA_DK
    hmask = (head_of_k[None, :] == np.arange(GLA_HEADS)[:, None]).astype(np.float32)
    bd = (head_of_k[:, None] == (np.arange(GLA_W) // GLA_DV)[None, :]).astype(np.float32)
    return sel, msk, hmask.reshape(GLA_HEADS, 1, GLA_KW), bd


def _gla_kernel(q_ref, k_ref, v_ref, g_ref, gate_ref, ong_ref, sel_ref, msk_ref, hm_ref, bd_ref,
                o_ref, state_ref):
    c, nl = GLA_CHUNK, GLA_LEVELS

    @pl.when(pl.program_id(1) == 0)
    def _():
        state_ref[...] = jnp.zeros_like(state_ref)

    g = g_ref[0]
    g1 = g.astype(BF16)
    r1 = g - g1.astype(F32)
    g2 = r1.astype(BF16)
    g3 = (r1 - g2.astype(F32)).astype(BF16)
    sel = sel_ref[...]
    e_all = _dot(sel, g1) + _dot(sel, g2) + _dot(sel, g3)
    cum = e_all[nl * c:(nl + 1) * c, :]

    q = q_ref[0] * (GLA_DK ** -0.5)
    k = k_ref[0]
    v = v_ref[0]

    def stacked_k(kl):
        return jnp.concatenate([(kl * hm_ref[hh]).astype(BF16) for hh in range(GLA_HEADS)], axis=0)

    att = msk_ref[nl] * _dot_nt(q.astype(BF16), stacked_k(k))
    for l in range(nl):
        e = jnp.exp(e_all[l * c:(l + 1) * c, :])
        att = att + msk_ref[l] * _dot_nt((q * e).astype(BF16), stacked_k(k * e))

    state = state_ref[...]
    o = _dot((q * jnp.exp(cum)).astype(BF16), state.astype(BF16))
    intra = [_dot(att[:, hh * c:(hh + 1) * c].astype(BF16), v[:, hh * GLA_DV:(hh + 1) * GLA_DV])
             for hh in range(GLA_HEADS)]
    o = o + jnp.concatenate(intra, axis=1)

    last = cum[c - 1:c, :]
    k_dec_t = (k * jnp.exp(last - cum)).T
    decay_t = jnp.broadcast_to(jnp.exp(last), (c, GLA_KW)).T
    reps = GLA_W // c
    state_ref[...] = (state * jnp.concatenate([decay_t] * reps, axis=1)
                      + bd_ref[...] * _dot(k_dec_t.astype(BF16), v))

    normed = _head_rmsnorm(o, ong_ref[...])
    o_ref[0] = (jnp.concatenate(normed, axis=1) * gate_ref[0]).astype(BF16)


def _gla(glq, glk, glv, gdec, gate, on_g):
    b, s, _ = glq.shape
    sel, msk, hm, bd = _gla_tables()
    c = GLA_CHUNK
    blk = lambda w: pl.BlockSpec((1, c, w), lambda bi, j: (bi, j, 0))
    const2 = lambda bi, j: (0, 0)
    const3 = lambda bi, j: (0, 0, 0)
    return pl.pallas_call(
        _gla_kernel,
        grid=(b, s // c),
        in_specs=[
            blk(GLA_KW), blk(GLA_KW), blk(GLA_W), blk(GLA_KW), blk(GLA_W),
            pl.BlockSpec((1, GLA_DV), const2),
            pl.BlockSpec(sel.shape, const2),
            pl.BlockSpec(msk.shape, const3),
            pl.BlockSpec(hm.shape, const3),
            pl.BlockSpec(bd.shape, const2),
        ],
        out_specs=blk(GLA_W),
        out_shape=jax.ShapeDtypeStruct((b, s, GLA_W), BF16),
        scratch_shapes=[pltpu.VMEM((GLA_KW, GLA_W), F32)],
        compiler_params=pltpu.CompilerParams(
            dimension_semantics=("arbitrary", "arbitrary"), vmem_limit_bytes=VMEM_LIMIT_BYTES),
        name="gla",
    )(glq, glk, glv, gdec, gate, on_g, jnp.asarray(sel, BF16), jnp.asarray(msk), jnp.asarray(hm),
      jnp.asarray(bd))


def _sb_kernel(q_ref, k_ref, v_ref, gate_ref, tri_ref, o_ref):
    blk = SB_BLOCK
    i = pl.program_id(2)
    q = q_ref[0]
    tri = tri_ref[...]

    def tile(kb, carry, acc, valid):
        start = pl.multiple_of(kb * blk, blk)
        kblk = k_ref[0, pl.ds(start, blk), :]
        vblk = v_ref[0, pl.ds(start, blk), :]
        z = _dot_nt(q, kblk)
        sp = _softplus(z)
        if valid is not None:
            sp = jnp.where(valid, sp, 0.0)
        hi = sp.astype(BF16)
        lo = (sp - hi.astype(F32)).astype(BF16)
        suffix = _dot(hi, tri) + _dot(lo, tri)
        w = jnp.exp(z - suffix - carry)
        if valid is not None:
            w = jnp.where(valid, w, 0.0)
        acc = acc + _dot(w.astype(BF16), vblk)
        return carry + suffix[:, 0:1], acc

    rows = lax.broadcasted_iota(jnp.int32, (blk, blk), 0)
    cols = lax.broadcasted_iota(jnp.int32, (blk, blk), 1)
    carry0 = jnp.zeros((blk, 1), F32)
    acc0 = jnp.zeros((blk, SB_HD), F32)
    carry, acc = tile(i, carry0, acc0, cols < rows)

    def body(t, st):
        return tile(i - 1 - t, st[0], st[1], None)

    carry, acc = lax.fori_loop(0, i, body, (carry, acc))
    o_ref[0] = (acc * gate_ref[0]).astype(BF16)


def _stick_breaking(q, k, v, gate):
    b, s, _ = q.shape
    blk = SB_BLOCK
    tri = (np.arange(blk)[:, None] >= np.arange(blk)[None, :]).astype(np.float32)
    qspec = pl.BlockSpec((1, blk, SB_HD), lambda bi, h, i: (bi, i, h))
    kvspec = pl.BlockSpec((1, s, SB_HD), lambda bi, h, i: (bi, 0, h))
    return pl.pallas_call(
        _sb_kernel,
        grid=(b, SB_HEADS, s // blk),
        in_specs=[qspec, kvspec, kvspec, qspec, pl.BlockSpec((blk, blk), lambda bi, h, i: (0, 0))],
        out_specs=qspec,
        out_shape=jax.ShapeDtypeStruct((b, s, SB_W), BF16),
        compiler_params=pltpu.CompilerParams(
            dimension_semantics=("parallel", "parallel", "arbitrary"),
            vmem_limit_bytes=VMEM_LIMIT_BYTES),
        name="stick_breaking",
    )(q, k, v, gate, jnp.asarray(tri, BF16))


def _out_proj_kernel(x_ref, mc_ref, ms_ref, mg_ref, w_ref, o_ref):
    cw = mc_ref.shape[1]
    y = _dot(mc_ref[...], w_ref[0:cw, :])
    y = y + _dot(ms_ref[...], w_ref[cw:cw + SB_W, :])
    y = y + _dot(mg_ref[...], w_ref[cw + SB_W:cw + SB_W + GLA_W, :])
    o_ref[...] = x_ref[...] + y


def _out_proj(x2, mixc, mixs, mixg, w_out):
    m, d = x2.shape
    row = lambda i: (i, 0)
    return pl.pallas_call(
        _out_proj_kernel,
        grid=(m // ROW_TILE,),
        in_specs=[
            pl.BlockSpec((ROW_TILE, d), row),
            pl.BlockSpec((ROW_TILE, mixc.shape[1]), row),
            pl.BlockSpec((ROW_TILE, SB_W), row),
            pl.BlockSpec((ROW_TILE, GLA_W), row),
            pl.BlockSpec(w_out.shape, lambda i: (0, 0), pipeline_mode=pl.Buffered(1)),
        ],
        out_specs=pl.BlockSpec((ROW_TILE, d), row),
        out_shape=jax.ShapeDtypeStruct((m, d), F32),
        compiler_params=pltpu.CompilerParams(
            dimension_semantics=("parallel",), vmem_limit_bytes=VMEM_LIMIT_BYTES),
        name="out_proj",
    )(x2, mixc, mixs, mixg, w_out)


def _layer(x2, batch, seq, norm_g, w_in, conv_w, conv_b, sb_qn_g, sb_kn_g, gla_w_up, gla_b_up,
           gla_on_g, w_out):
    d = x2.shape[1]
    cw = conv_w.shape[1]
    a_end = 4 * cw
    b_end = a_end + 4 * SB_W
    w_b = w_in[:, a_end:b_end].astype(BF16)
    n_gla = w_in.shape[1] - b_end
    pad = LANES - GLA_LR
    w_a = jnp.concatenate(
        [w_in[:, :a_end], w_in[:, b_end:], jnp.zeros((d, pad), w_in.dtype)], axis=1).astype(BF16)
    assert n_gla == 2 * GLA_KW + 2 * GLA_W + GLA_LR
    wup = jnp.concatenate([gla_w_up, jnp.zeros((pad, GLA_KW), gla_w_up.dtype)], axis=0).astype(BF16)

    row2 = lambda a: a.reshape(1, -1)
    q, k, v, sgate = _attn_proj(x2, row2(norm_g), w_b, row2(sb_qn_g), row2(sb_kn_g))
    mixc, glq, glk, glv, ggate, gdec = _conv_gla_proj(
        x2, row2(norm_g), w_a, conv_w, row2(conv_b), wup, row2(gla_b_up), batch=batch, seq=seq)

    r3 = lambda a: a.reshape(batch, seq, a.shape[-1])
    mixg = _gla(r3(glq), r3(glk), r3(glv), r3(gdec), r3(ggate), row2(gla_on_g))
    mixs = _stick_breaking(r3(q), r3(k), r3(v), r3(sgate))
    m = batch * seq
    return _out_proj(x2, mixc, mixs.reshape(m, SB_W), mixg.reshape(m, GLA_W), w_out.astype(BF16))


def kernel(x, norm_g, w_in, conv_w, conv_b, sb_qn_g, sb_kn_g, gla_w_up, gla_b_up, gla_on_g, w_out):
    batch, seq, d = x.shape
    assert seq % ROW_TILE == 0 and seq % SB_BLOCK == 0 and seq % GLA_CHUNK == 0
    h = x.reshape(batch * seq, d)
    for l in range(norm_g.shape[0]):
        h = _layer(h, batch, seq, norm_g[l], w_in[l], conv_w[l], conv_b[l], sb_qn_g[l], sb_kn_g[l],
                   gla_w_up[l], gla_b_up[l], gla_on_g[l], w_out[l])
    return h.reshape(batch, seq, d)
```

```python
import functools

import numpy as np
import jax
import jax.numpy as jnp
from jax import lax
from jax.experimental import pallas as pl
from jax.experimental.pallas import tpu as pltpu

SB_HEADS = 8
SB_HD = 128
SB_W = SB_HEADS * SB_HD
GLA_HEADS = 4
GLA_DV = 128
GLA_DK = 64
GLA_W = GLA_HEADS * GLA_DV
GLA_KW = GLA_HEADS * GLA_DK
GLA_LR = 16
GLA_TAU = 16.0
CONV_K = 3
EPS = 1e-6

LANES = 128
SUBLANES = 8
VMEM_LIMIT_BYTES = 56 * 1024 * 1024

ROW_TILE = 512
COL_CHUNK = 512
GLA_CHUNK = 128
GLA_LEVELS = 7
SB_BLOCK = 256
SB_SUBTILES = 4
MASKED = -1e30
LOG2_E = 1.4426950408889634

F32 = jnp.float32
BF16 = jnp.bfloat16


def _dot(a, b):
    return jnp.dot(a, b, preferred_element_type=F32)


def _dot_nt(a, b):
    return lax.dot_general(a, b, (((1,), (1,)), ((), ())), preferred_element_type=F32)


def _silu(a):
    return a / (1.0 + jnp.exp(-a))


def _softplus(z):
    return jnp.maximum(z, 0.0) + jnp.log(1.0 + jnp.exp(-jnp.abs(z)))


def _normed_input(x_ref, ng_ref):
    xf = x_ref[...]
    ms = jnp.mean(xf * xf, axis=-1, keepdims=True)
    return (xf * lax.rsqrt(ms + EPS) * ng_ref[...]).astype(BF16)


def _head_rmsnorm(a, gain):
    outs = []
    for j in range(a.shape[1] // LANES):
        aj = a[:, j * LANES:(j + 1) * LANES]
        ms = jnp.mean(aj * aj, axis=-1, keepdims=True)
        outs.append(aj * lax.rsqrt(ms + EPS) * gain)
    return outs


def _attn_proj_kernel(x_ref, ng_ref, w_ref, qg_ref, kg_ref, q_ref, k_ref, v_ref, g_ref):
    h = _normed_input(x_ref, ng_ref)
    q_scale = SB_HD ** -0.5 * LOG2_E
    n_chunks = SB_W // COL_CHUNK
    for c in range(n_chunks):
        cols = slice(c * COL_CHUNK, (c + 1) * COL_CHUNK)
        qa = _dot(h, w_ref[:, c * COL_CHUNK:(c + 1) * COL_CHUNK])
        for j, qn in enumerate(_head_rmsnorm(qa, qg_ref[...])):
            q_ref[:, c * COL_CHUNK + j * LANES:c * COL_CHUNK + (j + 1) * LANES] = (qn * q_scale).astype(BF16)
        ka = _dot(h, w_ref[:, SB_W + c * COL_CHUNK:SB_W + (c + 1) * COL_CHUNK])
        for j, kn in enumerate(_head_rmsnorm(ka, kg_ref[...])):
            k_ref[:, c * COL_CHUNK + j * LANES:c * COL_CHUNK + (j + 1) * LANES] = kn.astype(BF16)
        va = _dot(h, w_ref[:, 2 * SB_W + c * COL_CHUNK:2 * SB_W + (c + 1) * COL_CHUNK])
        v_ref[:, cols] = va.astype(BF16)
        ga = _dot(h, w_ref[:, 3 * SB_W + c * COL_CHUNK:3 * SB_W + (c + 1) * COL_CHUNK])
        g_ref[:, cols] = _silu(ga)


def _attn_proj(x2, norm_g, w_b, qn_g, kn_g):
    m, d = x2.shape
    row = lambda i: (i, 0)
    const = lambda i: (0, 0)
    return pl.pallas_call(
        _attn_proj_kernel,
        grid=(m // ROW_TILE,),
        in_specs=[
            pl.BlockSpec((ROW_TILE, d), row),
            pl.BlockSpec((1, d), const),
            pl.BlockSpec((d, 4 * SB_W), const, pipeline_mode=pl.Buffered(1)),
            pl.BlockSpec((1, SB_HD), const),
            pl.BlockSpec((1, SB_HD), const),
        ],
        out_specs=[pl.BlockSpec((ROW_TILE, SB_W), row)] * 4,
        out_shape=[
            jax.ShapeDtypeStruct((m, SB_W), BF16),
            jax.ShapeDtypeStruct((m, SB_W), BF16),
            jax.ShapeDtypeStruct((m, SB_W), BF16),
            jax.ShapeDtypeStruct((m, SB_W), F32),
        ],
        compiler_params=pltpu.CompilerParams(
            dimension_semantics=("parallel",), vmem_limit_bytes=VMEM_LIMIT_BYTES),
        name="attn_proj",
    )(x2, norm_g, w_b, qn_g, kn_g)


def _conv_gla_proj_kernel(x_ref, ng_ref, w_ref, cw_ref, cb_ref, wup_ref, bup_ref,
                          mixc_ref, glq_ref, glk_ref, glv_ref, gate_ref, gdec_ref, ubuf_ref,
                          *, conv_w):
    j = pl.program_id(1)
    tm = x_ref.shape[0]
    h = _normed_input(x_ref, ng_ref)

    cvh = _dot(h, w_ref[:, 0:conv_w])
    cvb = _dot(h, w_ref[:, conv_w:2 * conv_w])
    cvc = _dot(h, w_ref[:, 2 * conv_w:3 * conv_w])
    cvg = _dot(h, w_ref[:, 3 * conv_w:4 * conv_w])
    u = cvc * cvh
    halo = SUBLANES

    @pl.when(j == 0)
    def _():
        ubuf_ref[0:halo, :] = jnp.zeros((halo, conv_w), F32)

    ubuf_ref[halo:halo + tm, :] = u
    um1 = ubuf_ref[halo - 1:halo - 1 + tm, :]
    um2 = ubuf_ref[halo - 2:halo - 2 + tm, :]
    y = cw_ref[0:1, :] * um2 + cw_ref[1:2, :] * um1 + cw_ref[2:3, :] * u + cb_ref[...]
    mixc_ref[...] = (cvb * y * _silu(cvg)).astype(BF16)
    ubuf_ref[0:halo, :] = u[tm - halo:tm, :]

    base = 4 * conv_w
    glq_ref[...] = _dot(h, w_ref[:, base:base + GLA_KW])
    glk_ref[...] = _dot(h, w_ref[:, base + GLA_KW:base + 2 * GLA_KW])
    base += 2 * GLA_KW
    glv_ref[...] = _dot(h, w_ref[:, base:base + GLA_W]).astype(BF16)
    gate_ref[...] = _silu(_dot(h, w_ref[:, base + GLA_W:base + 2 * GLA_W]))
    base += 2 * GLA_W
    lr = _dot(h, w_ref[:, base:base + LANES])
    pre = _dot(lr.astype(BF16), wup_ref[...]) + bup_ref[...]
    log_sig = jnp.minimum(pre, 0.0) - jnp.log(1.0 + jnp.exp(-jnp.abs(pre)))
    gdec_ref[...] = log_sig / GLA_TAU


def _conv_gla_proj(x2, norm_g, w_a, conv_w, conv_b, wup, bup, *, batch, seq):
    m, d = x2.shape
    cw = conv_w.shape[1]
    n_tiles = seq // ROW_TILE
    row = lambda b, j: (b * n_tiles + j, 0)
    const = lambda b, j: (0, 0)
    ncols = w_a.shape[1]
    widths = [(cw, BF16), (GLA_KW, F32), (GLA_KW, F32), (GLA_W, BF16), (GLA_W, F32), (GLA_KW, F32)]
    return pl.pallas_call(
        functools.partial(_conv_gla_proj_kernel, conv_w=cw),
        grid=(batch, n_tiles),
        in_specs=[
            pl.BlockSpec((ROW_TILE, d), row),
            pl.BlockSpec((1, d), const),
            pl.BlockSpec((d, ncols), const, pipeline_mode=pl.Buffered(1)),
            pl.BlockSpec((CONV_K, cw), const),
            pl.BlockSpec((1, cw), const),
            pl.BlockSpec((LANES, GLA_KW), const),
            pl.BlockSpec((1, GLA_KW), const),
        ],
        out_specs=[pl.BlockSpec((ROW_TILE, w), row) for w, _ in widths],
        out_shape=[jax.ShapeDtypeStruct((m, w), dt) for w, dt in widths],
        scratch_shapes=[pltpu.VMEM((ROW_TILE + SUBLANES, cw), F32)],
        compiler_params=pltpu.CompilerParams(
            dimension_semantics=("arbitrary", "arbitrary"), vmem_limit_bytes=VMEM_LIMIT_BYTES),
        name="conv_gla_proj",
    )(x2, norm_g, w_a, conv_w, conv_b, wup, bup)


def _gla_tables():
    c, nl = GLA_CHUNK, GLA_LEVELS
    t = np.arange(c)
    sel = np.zeros((nl + 1, c, c), np.float32)
    msk = np.zeros((nl + 1, c, c), np.float32)
    for l in range(nl):
        up = ((t >> l) & 1) == 1
        lo_start = (t >> l) << l
        nxt = ((t >> l) + 1) << l
        for r in range(c):
            if up[r]:
                sel[l, r, lo_start[r]:r + 1] = 1.0
            else:
                sel[l, r, r + 1:nxt[r]] = 1.0
        same = (t[:, None] >> (l + 1)) == (t[None, :] >> (l + 1))
        msk[l] = (up[:, None] & ~up[None, :] & same).astype(np.float32)
    sel[nl] = (t[None, :] <= t[:, None]).astype(np.float32)
    msk[nl] = np.eye(c, dtype=np.float32)
    sel = sel.reshape((nl + 1) * c, c)
    msk = np.tile(msk, (1, 1, GLA_HEADS))
    head_of_k = np.arange(GLA_KW) // GLA_DK
    hmask = (head_of_k[None, :] == np.arange(GLA_HEADS)[:, None]).astype(np.float32)
    bd = (head_of_k[:, None] == (np.arange(GLA_W) // GLA_DV)[None, :]).astype(np.float32)
    return sel, msk, hmask.reshape(GLA_HEADS, 1, GLA_KW), bd


def _gla_kernel(q_ref, k_ref, v_ref, g_ref, gate_ref, ong_ref, sel_ref, msk_ref, hm_ref, bd_ref,
                o_ref, state_ref):
    c, nl = GLA_CHUNK, GLA_LEVELS

    @pl.when(pl.program_id(1) == 0)
    def _():
        state_ref[...] = jnp.zeros_like(state_ref)

    g = g_ref[0]
    g1 = g.astype(BF16)
    r1 = g - g1.astype(F32)
    g2 = r1.astype(BF16)
    g3 = (r1 - g2.astype(F32)).astype(BF16)
    sel = sel_ref[...]
    e_all = _dot(sel, g1) + _dot(sel, g2) + _dot(sel, g3)
    cum = e_all[nl * c:(nl + 1) * c, :]

    q = q_ref[0] * (GLA_DK ** -0.5)
    k = k_ref[0]
    v = v_ref[0]

    def stacked_k(kl):
        return jnp.concatenate([(kl * hm_ref[hh]).astype(BF16) for hh in range(GLA_HEADS)], axis=0)

    att = msk_ref[nl] * _dot_nt(q.astype(BF16), stacked_k(k))
    for l in range(nl):
        e = jnp.exp(e_all[l * c:(l + 1) * c, :])
        att = att + msk_ref[l] * _dot_nt((q * e).astype(BF16), stacked_k(k * e))

    state = state_ref[...]
    o = _dot((q * jnp.exp(cum)).astype(BF16), state.astype(BF16))
    intra = [_dot(att[:, hh * c:(hh + 1) * c].astype(BF16), v[:, hh * GLA_DV:(hh + 1) * GLA_DV])
             for hh in range(GLA_HEADS)]
    o = o + jnp.concatenate(intra, axis=1)

    last = cum[c - 1:c, :]
    k_dec_t = (k * jnp.exp(last - cum)).T
    decay_t = jnp.broadcast_to(jnp.exp(last), (c, GLA_KW)).T
    reps = GLA_W // c
    state_ref[...] = (state * jnp.concatenate([decay_t] * reps, axis=1)
                      + bd_ref[...] * _dot(k_dec_t.astype(BF16), v))

    normed = _head_rmsnorm(o, ong_ref[...])
    o_ref[0] = (jnp.concatenate(normed, axis=1) * gate_ref[0]).astype(BF16)


def _gla(glq, glk, glv, gdec, gate, on_g):
    b, s, _ = glq.shape
    sel, msk, hm, bd = _gla_tables()
    c = GLA_CHUNK
    blk = lambda w: pl.BlockSpec((1, c, w), lambda bi, j: (bi, j, 0))
    const2 = lambda bi, j: (0, 0)
    const3 = lambda bi, j: (0, 0, 0)
    return pl.pallas_call(
        _gla_kernel,
        grid=(b, s // c),
        in_specs=[
            blk(GLA_KW), blk(GLA_KW), blk(GLA_W), blk(GLA_KW), blk(GLA_W),
            pl.BlockSpec((1, GLA_DV), const2),
            pl.BlockSpec(sel.shape, const2),
            pl.BlockSpec(msk.shape, const3),
            pl.BlockSpec(hm.shape, const3),
            pl.BlockSpec(bd.shape, const2),
        ],
        out_specs=blk(GLA_W),
        out_shape=jax.ShapeDtypeStruct((b, s, GLA_W), BF16),
        scratch_shapes=[pltpu.VMEM((GLA_KW, GLA_W), F32)],
        compiler_params=pltpu.CompilerParams(
            dimension_semantics=("arbitrary", "arbitrary"), vmem_limit_bytes=VMEM_LIMIT_BYTES),
        name="gla",
    )(glq, glk, glv, gdec, gate, on_g, jnp.asarray(sel, BF16), jnp.asarray(msk), jnp.asarray(hm),
      jnp.asarray(bd))


def _sb_kernel(q_ref, k_ref, v_ref, gate_ref, tri_ref, o_ref, z_buf, zm_buf, sp_buf, d_buf, t_buf):
    blk, nsub = SB_BLOCK, SB_SUBTILES
    assert nsub % 2 == 0
    i = pl.program_id(2)
    tri = tri_ref[...]
    qs = [q_ref[0, a * blk:(a + 1) * blk, :] for a in range(nsub)]
    sub = lambda a: slice(a * blk, (a + 1) * blk)
    top = (i + 1) * nsub - 1
    n_blocks = top + 1

    def key_rows(ref, it_of_block):
        kb = jnp.clip(top - it_of_block, 0, top)
        return ref[0, pl.ds(pl.multiple_of(kb * blk, blk), blk), :]

    rows = lax.broadcasted_iota(jnp.int32, (blk, blk), 0)
    cols = lax.broadcasted_iota(jnp.int32, (blk, blk), 1)
    strictly_causal = cols < rows

    def step(it, st, par, static_it=None):
        st = list(st)
        live = lambda stage: static_it is None or static_it >= stage - 1
        kblk = key_rows(k_ref, it)
        for a in range(nsub):
            z_buf[par, sub(a), :] = _dot_nt(qs[a], kblk)
        for a in range(nsub if live(3) else 0):
            suffix = _dot(sp_buf[1 - par, sub(a), :], tri)
            d_buf[par, sub(a), :] = zm_buf[1 - par, sub(a), :] - suffix
            t_buf[par, sub(a), :] = jnp.broadcast_to(suffix[:, 0:1], (blk, LANES))
        vblk = key_rows(v_ref, it - 3)
        for a in range(nsub if live(4) else 0):
            carry = st[2 * a]
            w = jnp.exp2(d_buf[1 - par, sub(a), :] - jnp.concatenate([carry] * (blk // LANES), axis=1))
            st[2 * a + 1] = st[2 * a + 1] + _dot(w.astype(BF16), vblk)
            st[2 * a] = carry + t_buf[1 - par, sub(a), :]
        for a in range(nsub if live(2) else 0):
            vis = 1 if static_it is None else a - nsub + static_it
            if vis < 0:
                zm_buf[par, sub(a), :] = jnp.full((blk, blk), MASKED, F32)
                sp_buf[par, sub(a), :] = jnp.zeros((blk, blk), BF16)
                continue
            z = z_buf[1 - par, sub(a), :]
            sp = jnp.maximum(z, 0.0) + jnp.log2(1.0 + jnp.exp2(jnp.minimum(z, -z)))
            if vis == 0:
                sp = jnp.where(strictly_causal, sp, 0.0)
                z = jnp.where(strictly_causal, z, MASKED)
            zm_buf[par, sub(a), :] = z
            sp_buf[par, sub(a), :] = sp.astype(BF16)
        return tuple(st)

    st = []
    for a in range(nsub):
        st += [jnp.zeros((blk, LANES), F32), jnp.zeros((blk, SB_HD), F32)]
    st = tuple(st)
    for it in range(nsub + 1):
        st = step(it, st, it % 2, static_it=it)

    def double_step(j, st):
        it = nsub + 1 + 2 * j
        st = step(it, st, (nsub + 1) % 2)
        return step(it + 1, st, nsub % 2)

    st = lax.fori_loop(0, (n_blocks + 2 - nsub) // 2, double_step, st)
    for a in range(nsub):
        o_ref[0, sub(a), :] = (st[2 * a + 1] * gate_ref[0, sub(a), :]).astype(BF16)


def _stick_breaking(q, k, v, gate):
    b, s, _ = q.shape
    blk = SB_BLOCK
    qblk = SB_BLOCK * SB_SUBTILES
    tri = (np.arange(blk)[:, None] >= np.arange(blk)[None, :]).astype(np.float32)
    qspec = pl.BlockSpec((1, qblk, SB_HD), lambda bi, h, i: (bi, i, h))
    kvspec = pl.BlockSpec((1, s, SB_HD), lambda bi, h, i: (bi, 0, h))
    return pl.pallas_call(
        _sb_kernel,
        grid=(b, SB_HEADS, s // qblk),
        in_specs=[qspec, kvspec, kvspec, qspec, pl.BlockSpec((blk, blk), lambda bi, h, i: (0, 0))],
        out_specs=qspec,
        out_shape=jax.ShapeDtypeStruct((b, s, SB_W), BF16),
        scratch_shapes=[pltpu.VMEM((2, qblk, blk), F32), pltpu.VMEM((2, qblk, blk), F32),
                        pltpu.VMEM((2, qblk, blk), BF16), pltpu.VMEM((2, qblk, blk), F32),
                        pltpu.VMEM((2, qblk, LANES), F32)],
        compiler_params=pltpu.CompilerParams(
            dimension_semantics=("parallel", "parallel", "arbitrary"),
            vmem_limit_bytes=VMEM_LIMIT_BYTES),
        name="stick_breaking",
    )(q, k, v, gate, jnp.asarray(tri, BF16))


def _out_proj_kernel(x_ref, mc_ref, ms_ref, mg_ref, w_ref, o_ref):
    cw = mc_ref.shape[1]
    y = _dot(mc_ref[...], w_ref[0:cw, :])
    y = y + _dot(ms_ref[...], w_ref[cw:cw + SB_W, :])
    y = y + _dot(mg_ref[...], w_ref[cw + SB_W:cw + SB_W + GLA_W, :])
    o_ref[...] = x_ref[...] + y


def _out_proj(x2, mixc, mixs, mixg, w_out):
    m, d = x2.shape
    row = lambda i: (i, 0)
    return pl.pallas_call(
        _out_proj_kernel,
        grid=(m // ROW_TILE,),
        in_specs=[
            pl.BlockSpec((ROW_TILE, d), row),
            pl.BlockSpec((ROW_TILE, mixc.shape[1]), row),
            pl.BlockSpec((ROW_TILE, SB_W), row),
            pl.BlockSpec((ROW_TILE, GLA_W), row),
            pl.BlockSpec(w_out.shape, lambda i: (0, 0), pipeline_mode=pl.Buffered(1)),
        ],
        out_specs=pl.BlockSpec((ROW_TILE, d), row),
        out_shape=jax.ShapeDtypeStruct((m, d), F32),
        compiler_params=pltpu.CompilerParams(
            dimension_semantics=("parallel",), vmem_limit_bytes=VMEM_LIMIT_BYTES),
        name="out_proj",
    )(x2, mixc, mixs, mixg, w_out)


def _layer(x2, batch, seq, norm_g, w_in, conv_w, conv_b, sb_qn_g, sb_kn_g, gla_w_up, gla_b_up,
           gla_on_g, w_out):
    d = x2.shape[1]
    cw = conv_w.shape[1]
    a_end = 4 * cw
    b_end = a_end + 4 * SB_W
    w_b = w_in[:, a_end:b_end].astype(BF16)
    n_gla = w_in.shape[1] - b_end
    pad = LANES - GLA_LR
    w_a = jnp.concatenate(
        [w_in[:, :a_end], w_in[:, b_end:], jnp.zeros((d, pad), w_in.dtype)], axis=1).astype(BF16)
    assert n_gla == 2 * GLA_KW + 2 * GLA_W + GLA_LR
    wup = jnp.concatenate([gla_w_up, jnp.zeros((pad, GLA_KW), gla_w_up.dtype)], axis=0).astype(BF16)

    row2 = lambda a: a.reshape(1, -1)
    q, k, v, sgate = _attn_proj(x2, row2(norm_g), w_b, row2(sb_qn_g), row2(sb_kn_g))
    mixc, glq, glk, glv, ggate, gdec = _conv_gla_proj(
        x2, row2(norm_g), w_a, conv_w, row2(conv_b), wup, row2(gla_b_up), batch=batch, seq=seq)

    r3 = lambda a: a.reshape(batch, seq, a.shape[-1])
    mixg = _gla(r3(glq), r3(glk), r3(glv), r3(gdec), r3(ggate), row2(gla_on_g))
    mixs = _stick_breaking(r3(q), r3(k), r3(v), r3(sgate))
    m = batch * seq
    return _out_proj(x2, mixc, mixs.reshape(m, SB_W), mixg.reshape(m, GLA_W), w_out.astype(BF16))


def kernel(x, norm_g, w_in, conv_w, conv_b, sb_qn_g, sb_kn_g, gla_w_up, gla_b_up, gla_on_g, w_out):
    batch, seq, d = x.shape
    assert seq % ROW_TILE == 0 and seq % (SB_BLOCK * SB_SUBTILES) == 0 and seq % GLA_CHUNK == 0
    h = x.reshape(batch * seq, d)
    for l in range(norm_g.shape[0]):
        h = _layer(h, batch, seq, norm_g[l], w_in[l], conv_w[l], conv_b[l], sb_qn_g[l], sb_kn_g[l],
                   gla_w_up[l], gla_b_up[l], gla_on_g[l], w_out[l])
    return h.reshape(batch, seq, d)
```

```python
import functools

import numpy as np
import jax
import jax.numpy as jnp
from jax import lax
from jax.experimental import pallas as pl
from jax.experimental.pallas import tpu as pltpu

SB_HEADS = 8
SB_HD = 128
SB_W = SB_HEADS * SB_HD
GLA_HEADS = 4
GLA_DV = 128
GLA_DK = 64
GLA_W = GLA_HEADS * GLA_DV
GLA_KW = GLA_HEADS * GLA_DK
GLA_LR = 16
GLA_TAU = 16.0
CONV_K = 3
EPS = 1e-6

LANES = 128
SUBLANES = 8
VMEM_LIMIT_BYTES = 56 * 1024 * 1024

ROW_TILE = 512
COL_CHUNK = 512
GLA_CHUNK = 128
GLA_STEP_CHUNKS = 4
GLA_LEVELS = 7
SB_BLOCK = 256
SB_SUBTILES = 8
MASKED = -1e30
EXHAUSTED = 1e30
F32_EXP2_UNDERFLOW = 150.0
LOG2_E = 1.4426950408889634

F32 = jnp.float32
BF16 = jnp.bfloat16


def _dot(a, b):
    return jnp.dot(a, b, preferred_element_type=F32)


def _dot_nt(a, b):
    return lax.dot_general(a, b, (((1,), (1,)), ((), ())), preferred_element_type=F32)


def _silu(a):
    return a / (1.0 + jnp.exp(-a))


def _normed_input(x_ref, ng_ref):
    xf = x_ref[...]
    ms = jnp.mean(xf * xf, axis=-1, keepdims=True)
    return (xf * lax.rsqrt(ms + EPS) * ng_ref[...]).astype(BF16)


def _head_rmsnorm(a, gain):
    outs = []
    for j in range(a.shape[1] // LANES):
        aj = a[:, j * LANES:(j + 1) * LANES]
        ms = jnp.mean(aj * aj, axis=-1, keepdims=True)
        outs.append(aj * lax.rsqrt(ms + EPS) * gain)
    return outs


def _attn_proj_kernel(x_ref, ng_ref, wqk_ref, wvg_ref, qg_ref, kg_ref, q_ref, k_ref, v_ref, g_ref):
    h = _normed_input(x_ref, ng_ref)
    q_scale = SB_HD ** -0.5 * LOG2_E
    n_chunks = SB_W // COL_CHUNK
    for c in range(n_chunks):
        cols = slice(c * COL_CHUNK, (c + 1) * COL_CHUNK)
        qa = _dot(h, wqk_ref[:, c * COL_CHUNK:(c + 1) * COL_CHUNK])
        for j, qn in enumerate(_head_rmsnorm(qa, qg_ref[...])):
            q_ref[:, c * COL_CHUNK + j * LANES:c * COL_CHUNK + (j + 1) * LANES] = (qn * q_scale).astype(BF16)
        ka = _dot(h, wqk_ref[:, SB_W + c * COL_CHUNK:SB_W + (c + 1) * COL_CHUNK])
        for j, kn in enumerate(_head_rmsnorm(ka, kg_ref[...])):
            k_ref[:, c * COL_CHUNK + j * LANES:c * COL_CHUNK + (j + 1) * LANES] = kn.astype(BF16)
        va = _dot(h, wvg_ref[:, c * COL_CHUNK:(c + 1) * COL_CHUNK])
        v_ref[:, cols] = va.astype(BF16)
        ga = _dot(h, wvg_ref[:, SB_W + c * COL_CHUNK:SB_W + (c + 1) * COL_CHUNK])
        g_ref[:, cols] = _silu(ga)


def _attn_proj(x2, norm_g, w_in, layer, first_col, qn_g, kn_g):
    m, d = x2.shape
    row = lambda i: (i, 0)
    const = lambda i: (0, 0)
    wblk = 2 * SB_W
    assert first_col % wblk == 0
    wspec = lambda n: pl.BlockSpec((None, d, wblk), lambda i: (layer, 0, first_col // wblk + n),
                                   pipeline_mode=pl.Buffered(1))
    return pl.pallas_call(
        _attn_proj_kernel,
        grid=(m // ROW_TILE,),
        in_specs=[
            pl.BlockSpec((ROW_TILE, d), row),
            pl.BlockSpec((1, d), const),
            wspec(0),
            wspec(1),
            pl.BlockSpec((1, SB_HD), const),
            pl.BlockSpec((1, SB_HD), const),
        ],
        out_specs=[pl.BlockSpec((ROW_TILE, SB_W), row)] * 4,
        out_shape=[
            jax.ShapeDtypeStruct((m, SB_W), BF16),
            jax.ShapeDtypeStruct((m, SB_W), BF16),
            jax.ShapeDtypeStruct((m, SB_W), BF16),
            jax.ShapeDtypeStruct((m, SB_W), F32),
        ],
        compiler_params=pltpu.CompilerParams(
            dimension_semantics=("parallel",), vmem_limit_bytes=VMEM_LIMIT_BYTES),
        name="attn_proj",
    )(x2, norm_g, w_in, w_in, qn_g, kn_g)


def _conv_gla_proj_kernel(x_ref, ng_ref, wc_ref, wg_ref, cw_ref, cb_ref, wup_ref, bup_ref,
                          mixc_ref, glq_ref, glk_ref, glv_ref, gate_ref, gdec_ref, ubuf_ref,
                          *, conv_w):
    tm = x_ref.shape[0]
    halo = SUBLANES

    @pl.when(pl.program_id(1) == 0)
    def _():
        ubuf_ref[0:halo, :] = jnp.zeros((halo, conv_w), F32)

    h = _normed_input(x_ref, ng_ref)
    cvh = _dot(h, wc_ref[:, 0:conv_w])
    cvb = _dot(h, wc_ref[:, conv_w:2 * conv_w])
    cvc = _dot(h, wc_ref[:, 2 * conv_w:3 * conv_w])
    cvg = _dot(h, wc_ref[:, 3 * conv_w:4 * conv_w])
    u = cvc * cvh
    ubuf_ref[halo:halo + tm, :] = u
    um1 = ubuf_ref[halo - 1:halo - 1 + tm, :]
    um2 = ubuf_ref[halo - 2:halo - 2 + tm, :]
    y = cw_ref[0:1, :] * um2 + cw_ref[1:2, :] * um1 + cw_ref[2:3, :] * u + cb_ref[...]
    mixc_ref[...] = (cvb * y * _silu(cvg)).astype(BF16)
    ubuf_ref[0:halo, :] = u[tm - halo:tm, :]

    base = 0
    glq_ref[...] = _dot(h, wg_ref[:, base:base + GLA_KW])
    glk_ref[...] = _dot(h, wg_ref[:, base + GLA_KW:base + 2 * GLA_KW])
    base += 2 * GLA_KW
    glv_ref[...] = _dot(h, wg_ref[:, base:base + GLA_W]).astype(BF16)
    gate_ref[...] = _silu(_dot(h, wg_ref[:, base + GLA_W:base + 2 * GLA_W]))
    base += 2 * GLA_W
    lr = _dot(h, wg_ref[:, base:base + LANES])
    pre = _dot(lr.astype(BF16), wup_ref[...]) + bup_ref[...]
    log_sig = jnp.minimum(pre, 0.0) - jnp.log(1.0 + jnp.exp(-jnp.abs(pre)))
    gdec_ref[...] = log_sig / GLA_TAU


def _conv_gla_proj(x2, norm_g, w_in, w_gla, layer, conv_w, conv_b, wup, bup, *, batch, seq):
    m, d = x2.shape
    cw = conv_w.shape[1]
    n_tiles = seq // ROW_TILE
    row = lambda b, j: (b * n_tiles + j, 0)
    const = lambda b, j: (0, 0)
    wconst = lambda b, j: (layer, 0, 0)
    widths = [(cw, BF16), (GLA_KW, F32), (GLA_KW, F32), (GLA_W, BF16), (GLA_W, F32), (GLA_KW, F32)]
    return pl.pallas_call(
        functools.partial(_conv_gla_proj_kernel, conv_w=cw),
        grid=(batch, n_tiles),
        in_specs=[
            pl.BlockSpec((ROW_TILE, d), row),
            pl.BlockSpec((1, d), const),
            pl.BlockSpec((None, d, 4 * cw), wconst, pipeline_mode=pl.Buffered(1)),
            pl.BlockSpec((None, d, w_gla.shape[2]), wconst, pipeline_mode=pl.Buffered(1)),
            pl.BlockSpec((CONV_K, cw), const),
            pl.BlockSpec((1, cw), const),
            pl.BlockSpec((LANES, GLA_KW), const),
            pl.BlockSpec((1, GLA_KW), const),
        ],
        out_specs=[pl.BlockSpec((ROW_TILE, w), row) for w, _ in widths],
        out_shape=[jax.ShapeDtypeStruct((m, w), dt) for w, dt in widths],
        scratch_shapes=[pltpu.VMEM((ROW_TILE + SUBLANES, cw), F32)],
        compiler_params=pltpu.CompilerParams(
            dimension_semantics=("arbitrary", "arbitrary"), vmem_limit_bytes=VMEM_LIMIT_BYTES),
        name="conv_gla_proj",
    )(x2, norm_g, w_in, w_gla, conv_w, conv_b, wup, bup)


def _gla_tables():
    c, nl = GLA_CHUNK, GLA_LEVELS
    t = np.arange(c)
    sel = np.zeros((nl + 1, c, c), np.float32)
    msk = np.zeros((nl + 1, c, c), np.float32)
    for l in range(nl):
        up = ((t >> l) & 1) == 1
        lo_start = (t >> l) << l
        nxt = ((t >> l) + 1) << l
        for r in range(c):
            if up[r]:
                sel[l, r, lo_start[r]:r + 1] = 1.0
            else:
                sel[l, r, r + 1:nxt[r]] = 1.0
        same = (t[:, None] >> (l + 1)) == (t[None, :] >> (l + 1))
        msk[l] = (up[:, None] & ~up[None, :] & same).astype(np.float32)
    sel[nl] = (t[None, :] <= t[:, None]).astype(np.float32)
    msk[nl] = np.eye(c, dtype=np.float32)
    sel = sel.reshape((nl + 1) * c, c)
    msk = np.tile(msk, (1, 1, GLA_HEADS))
    head_of_k = np.arange(GLA_KW) // GLA_DK
    hmask = (head_of_k[None, :] == np.arange(GLA_HEADS)[:, None]).astype(np.float32)
    bd = (head_of_k[:, None] == (np.arange(GLA_W) // GLA_DV)[None, :]).astype(np.float32)
    return sel, msk, hmask.reshape(GLA_HEADS, 1, GLA_KW), bd


def _gla_kernel(q_ref, k_ref, v_ref, g_ref, gate_ref, ong_ref, sel_ref, msk_ref, hm_ref, bd_ref,
                o_ref, state_ref):
    c, nl = GLA_CHUNK, GLA_LEVELS

    @pl.when(pl.program_id(1) == 0)
    def _():
        state_ref[...] = jnp.zeros_like(state_ref)

    sel = sel_ref[...]

    def stacked_k(kl):
        return jnp.concatenate([(kl * hm_ref[hh]).astype(BF16) for hh in range(GLA_HEADS)], axis=0)

    chunks = range(GLA_STEP_CHUNKS)
    rows = [slice(ci * c, (ci + 1) * c) for ci in chunks]
    e_alls = []
    for ci in chunks:
        g = g_ref[0, rows[ci], :]
        g1 = g.astype(BF16)
        g2 = (g - g1.astype(F32)).astype(BF16)
        e_alls.append(_dot(sel, g1) + _dot(sel, g2))

    atts, q_decs, updates, decays = [], [], [], []
    for ci in chunks:
        e_all = e_alls[ci]
        cum = e_all[nl * c:(nl + 1) * c, :]
        q = q_ref[0, rows[ci], :] * (GLA_DK ** -0.5)
        k = k_ref[0, rows[ci], :]
        v = v_ref[0, rows[ci], :]
        att = msk_ref[nl] * _dot_nt(q.astype(BF16), stacked_k(k))
        for l in range(nl):
            e = jnp.exp(e_all[l * c:(l + 1) * c, :])
            att = att + msk_ref[l] * _dot_nt((q * e).astype(BF16), stacked_k(k * e))
        atts.append(att)
        q_decs.append((q * jnp.exp(cum)).astype(BF16))
        last = cum[c - 1:c, :]
        k_dec_t = (k * jnp.exp(last - cum)).T
        decay_t = jnp.broadcast_to(jnp.exp(last), (c, GLA_KW)).T
        decays.append(jnp.concatenate([decay_t] * (GLA_W // c), axis=1))
        updates.append(bd_ref[...] * _dot(k_dec_t.astype(BF16), v))

    state = state_ref[...]
    for ci in chunks:
        v = v_ref[0, rows[ci], :]
        o = _dot(q_decs[ci], state.astype(BF16))
        intra = [_dot(atts[ci][:, hh * c:(hh + 1) * c].astype(BF16), v[:, hh * GLA_DV:(hh + 1) * GLA_DV])
                 for hh in range(GLA_HEADS)]
        o = o + jnp.concatenate(intra, axis=1)
        state = state * decays[ci] + updates[ci]
        normed = _head_rmsnorm(o, ong_ref[...])
        o_ref[0, rows[ci], :] = (jnp.concatenate(normed, axis=1) * gate_ref[0, rows[ci], :]).astype(BF16)
    state_ref[...] = state


def _gla(glq, glk, glv, gdec, gate, on_g):
    b, s, _ = glq.shape
    sel, msk, hm, bd = _gla_tables()
    c = GLA_CHUNK
    rows = c * GLA_STEP_CHUNKS
    blk = lambda w: pl.BlockSpec((1, rows, w), lambda bi, j: (bi, j, 0))
    const2 = lambda bi, j: (0, 0)
    const3 = lambda bi, j: (0, 0, 0)
    return pl.pallas_call(
        _gla_kernel,
        grid=(b, s // rows),
        in_specs=[
            blk(GLA_KW), blk(GLA_KW), blk(GLA_W), blk(GLA_KW), blk(GLA_W),
            pl.BlockSpec((1, GLA_DV), const2),
            pl.BlockSpec(sel.shape, const2),
            pl.BlockSpec(msk.shape, const3),
            pl.BlockSpec(hm.shape, const3),
            pl.BlockSpec(bd.shape, const2),
        ],
        out_specs=blk(GLA_W),
        out_shape=jax.ShapeDtypeStruct((b, s, GLA_W), BF16),
        scratch_shapes=[pltpu.VMEM((GLA_KW, GLA_W), F32)],
        compiler_params=pltpu.CompilerParams(
            dimension_semantics=("arbitrary", "arbitrary"), vmem_limit_bytes=VMEM_LIMIT_BYTES),
        name="gla",
    )(glq, glk, glv, gdec, gate, on_g, jnp.asarray(sel, BF16), jnp.asarray(msk), jnp.asarray(hm),
      jnp.asarray(bd))


def _sb_kernel(q_ref, k_ref, v_ref, gate_ref, tri_ref, o_ref):
    blk, nsub = SB_BLOCK, SB_SUBTILES
    i = pl.program_id(2)
    tri = tri_ref[...]
    qs = [q_ref[0, a * blk:(a + 1) * blk, :] for a in range(nsub)]
    diag = [i * nsub + a for a in range(nsub)]
    rows = lax.broadcasted_iota(jnp.int32, (blk, blk), 0)
    cols = lax.broadcasted_iota(jnp.int32, (blk, blk), 1)
    strictly_causal = cols < rows

    def key_rows(ref, kb):
        return ref[0, pl.ds(pl.multiple_of(kb * blk, blk), blk), :]

    def visit(blocks, carries, accs, on_diagonal):
        carries, accs = list(carries), list(accs)
        zs, sps, suffixes, ws = ([None] * nsub for _ in range(4))

        def scores(a):
            zs[a] = _dot_nt(qs[a], key_rows(k_ref, blocks[a]))

        def softplus2(a):
            z = zs[a]
            sp = jnp.maximum(z, 0.0) + jnp.log2(1.0 + jnp.exp2(jnp.minimum(z, -z)))
            if on_diagonal:
                sp = jnp.where(strictly_causal, sp, 0.0)
                zs[a] = jnp.where(strictly_causal, z, MASKED)
            sps[a] = sp.astype(BF16)

        def suffix_sum(a):
            suffixes[a] = _dot(sps[a], tri)

        def weights(a):
            d = jnp.minimum(zs[a] - suffixes[a], 0.0)
            carry = jnp.concatenate([carries[a]] * (blk // LANES), axis=1)
            ws[a] = jnp.exp2(d - carry).astype(BF16)
            carries[a] = carries[a] + jnp.broadcast_to(suffixes[a][:, 0:1], (blk, LANES))

        def accumulate(a):
            accs[a] = accs[a] + _dot(ws[a], key_rows(v_ref, blocks[a]))

        stages = (scores, softplus2, suffix_sum, weights, accumulate)
        for t in range(nsub + len(stages) - 1):
            for a in range(nsub):
                if 0 <= t - a < len(stages):
                    stages[t - a](a)
        return carries, accs

    def unfinished(carries, next_step):
        lowest = None
        for a in range(nsub):
            c = jnp.min(carries[a].reshape(blk // SUBLANES, SUBLANES, LANES), axis=0)
            c = jnp.where(diag[a] - next_step < 0, EXHAUSTED, c)
            lowest = c if lowest is None else jnp.minimum(lowest, c)
        return jnp.min(lowest) < F32_EXP2_UNDERFLOW

    carries = [jnp.zeros((blk, LANES), F32) for _ in range(nsub)]
    accs = [jnp.zeros((blk, SB_HD), F32) for _ in range(nsub)]
    carries, accs = visit(diag, carries, accs, on_diagonal=True)

    def body(state):
        _, step, carries, accs = state
        carries = [jnp.where(diag[a] - step < 0, EXHAUSTED, carries[a]) for a in range(nsub)]
        blocks = [jnp.maximum(diag[a] - step, 0) for a in range(nsub)]
        carries, accs = visit(blocks, list(carries), list(accs), on_diagonal=False)
        return unfinished(carries, step + 1), step + 1, tuple(carries), tuple(accs)

    state = (unfinished(carries, 1), jnp.int32(1), tuple(carries), tuple(accs))
    _, _, _, accs = lax.while_loop(lambda state: state[0], body, state)
    for a in range(nsub):
        rows_a = slice(a * blk, (a + 1) * blk)
        o_ref[0, rows_a, :] = (accs[a] * gate_ref[0, rows_a, :]).astype(BF16)


def _stick_breaking(q, k, v, gate):
    b, s, _ = q.shape
    blk = SB_BLOCK
    qblk = SB_BLOCK * SB_SUBTILES
    tri = (np.arange(blk)[:, None] >= np.arange(blk)[None, :]).astype(np.float32)
    qspec = pl.BlockSpec((1, qblk, SB_HD), lambda bi, h, i: (bi, i, h))
    kvspec = pl.BlockSpec((1, s, SB_HD), lambda bi, h, i: (bi, 0, h))
    return pl.pallas_call(
        _sb_kernel,
        grid=(b, SB_HEADS, s // qblk),
        in_specs=[qspec, kvspec, kvspec, qspec, pl.BlockSpec((blk, blk), lambda bi, h, i: (0, 0))],
        out_specs=qspec,
        out_shape=jax.ShapeDtypeStruct((b, s, SB_W), BF16),
        compiler_params=pltpu.CompilerParams(
            dimension_semantics=("parallel", "parallel", "arbitrary"),
            vmem_limit_bytes=VMEM_LIMIT_BYTES),
        name="stick_breaking",
    )(q, k, v, gate, jnp.asarray(tri, BF16))


def _out_proj_kernel(x_ref, mc_ref, ms_ref, mg_ref, w_ref, o_ref):
    cw = mc_ref.shape[1]
    y = _dot(mc_ref[...], w_ref[0:cw, :])
    y = y + _dot(ms_ref[...], w_ref[cw:cw + SB_W, :])
    y = y + _dot(mg_ref[...], w_ref[cw + SB_W:cw + SB_W + GLA_W, :])
    o_ref[...] = x_ref[...] + y


def _out_proj(x2, mixc, mixs, mixg, w_out, layer):
    m, d = x2.shape
    row = lambda i: (i, 0)
    return pl.pallas_call(
        _out_proj_kernel,
        grid=(m // ROW_TILE,),
        in_specs=[
            pl.BlockSpec((ROW_TILE, d), row),
            pl.BlockSpec((ROW_TILE, mixc.shape[1]), row),
            pl.BlockSpec((ROW_TILE, SB_W), row),
            pl.BlockSpec((ROW_TILE, GLA_W), row),
            pl.BlockSpec((None,) + w_out.shape[1:], lambda i: (layer, 0, 0), pipeline_mode=pl.Buffered(1)),
        ],
        out_specs=pl.BlockSpec((ROW_TILE, d), row),
        out_shape=jax.ShapeDtypeStruct((m, d), F32),
        compiler_params=pltpu.CompilerParams(
            dimension_semantics=("parallel",), vmem_limit_bytes=VMEM_LIMIT_BYTES),
        name="out_proj",
    )(x2, mixc, mixs, mixg, w_out)


def _layer(x2, batch, seq, layer, w_in, w_gla, w_out, norm_g, conv_w, conv_b, sb_qn_g, sb_kn_g,
           gla_w_up, gla_b_up, gla_on_g):
    pad = LANES - GLA_LR
    wup = jnp.concatenate([gla_w_up, jnp.zeros((pad, GLA_KW), gla_w_up.dtype)], axis=0).astype(BF16)
    row2 = lambda a: a.reshape(1, -1)
    q, k, v, sgate = _attn_proj(x2, row2(norm_g), w_in, layer, 4 * conv_w.shape[1],
                                row2(sb_qn_g), row2(sb_kn_g))
    mixc, glq, glk, glv, ggate, gdec = _conv_gla_proj(
        x2, row2(norm_g), w_in, w_gla, layer, conv_w, row2(conv_b), wup, row2(gla_b_up),
        batch=batch, seq=seq)

    r3 = lambda a: a.reshape(batch, seq, a.shape[-1])
    mixg = _gla(r3(glq), r3(glk), r3(glv), r3(gdec), r3(ggate), row2(gla_on_g))
    mixs = _stick_breaking(r3(q), r3(k), r3(v), r3(sgate))
    m = batch * seq
    return _out_proj(x2, mixc, mixs.reshape(m, SB_W), mixg.reshape(m, GLA_W), w_out, layer)


def kernel(x, norm_g, w_in, conv_w, conv_b, sb_qn_g, sb_kn_g, gla_w_up, gla_b_up, gla_on_g, w_out):
    batch, seq, d = x.shape
    assert seq % ROW_TILE == 0 and seq % (SB_BLOCK * SB_SUBTILES) == 0
    assert seq % (GLA_CHUNK * GLA_STEP_CHUNKS) == 0
    gla_first = 4 * conv_w.shape[2] + 4 * SB_W
    assert w_in.shape[2] - gla_first == 2 * GLA_KW + 2 * GLA_W + GLA_LR
    w_in_b = w_in.astype(BF16)
    w_gla_b = jnp.pad(w_in[:, :, gla_first:], ((0, 0), (0, 0), (0, LANES - GLA_LR))).astype(BF16)
    w_out_b = w_out.astype(BF16)
    h = x.reshape(batch * seq, d)
    for l in range(norm_g.shape[0]):
        h = _layer(h, batch, seq, l, w_in_b, w_gla_b, w_out_b, norm_g[l], conv_w[l], conv_b[l],
                   sb_qn_g[l], sb_kn_g[l], gla_w_up[l], gla_b_up[l], gla_on_g[l])
    return h.reshape(batch, seq, d)
```

```python
import functools

import numpy as np
import jax
import jax.numpy as jnp
from jax import lax
from jax.experimental import pallas as pl
from jax.experimental.pallas import tpu as pltpu

SB_HEADS = 8
SB_HD = 128
SB_W = SB_HEADS * SB_HD
GLA_HEADS = 4
GLA_DV = 128
GLA_DK = 64
GLA_W = GLA_HEADS * GLA_DV
GLA_KW = GLA_HEADS * GLA_DK
GLA_LR = 16
GLA_TAU = 16.0
CONV_K = 3
EPS = 1e-6

LANES = 128
SUBLANES = 8
VMEM_LIMIT_BYTES = 56 * 1024 * 1024

ROW_TILE = 512
COL_CHUNK = 512
GLA_CHUNK = 128
GLA_STEP_CHUNKS = 4
GLA_LEVELS = 7
GLA_MATMUL_LEVELS = 3
SB_BLOCK = 256
SB_SUBTILES = 8
MASKED = -1e30
EXHAUSTED = 1e30
F32_EXP2_UNDERFLOW = 150.0
LOG2_E = 1.4426950408889634

F32 = jnp.float32
BF16 = jnp.bfloat16


def _dot(a, b):
    return jnp.dot(a, b, preferred_element_type=F32)


def _dot_nt(a, b):
    return lax.dot_general(a, b, (((1,), (1,)), ((), ())), preferred_element_type=F32)


def _silu(a):
    return a / (1.0 + jnp.exp(-a))


def _normed_input(x_ref, ng_ref):
    xf = x_ref[...]
    ms = jnp.mean(xf * xf, axis=-1, keepdims=True)
    return (xf * lax.rsqrt(ms + EPS) * ng_ref[...]).astype(BF16)


def _head_rmsnorm(a, gain):
    outs = []
    for j in range(a.shape[1] // LANES):
        aj = a[:, j * LANES:(j + 1) * LANES]
        ms = jnp.mean(aj * aj, axis=-1, keepdims=True)
        outs.append(aj * lax.rsqrt(ms + EPS) * gain)
    return outs


def _attn_proj_kernel(x_ref, ng_ref, wqk_ref, wvg_ref, qg_ref, kg_ref, q_ref, k_ref, v_ref, g_ref):
    h = _normed_input(x_ref, ng_ref)
    q_scale = SB_HD ** -0.5 * LOG2_E
    n_chunks = SB_W // COL_CHUNK
    for c in range(n_chunks):
        cols = slice(c * COL_CHUNK, (c + 1) * COL_CHUNK)
        qa = _dot(h, wqk_ref[:, c * COL_CHUNK:(c + 1) * COL_CHUNK])
        for j, qn in enumerate(_head_rmsnorm(qa, qg_ref[...])):
            q_ref[:, c * COL_CHUNK + j * LANES:c * COL_CHUNK + (j + 1) * LANES] = (qn * q_scale).astype(BF16)
        ka = _dot(h, wqk_ref[:, SB_W + c * COL_CHUNK:SB_W + (c + 1) * COL_CHUNK])
        for j, kn in enumerate(_head_rmsnorm(ka, kg_ref[...])):
            k_ref[:, c * COL_CHUNK + j * LANES:c * COL_CHUNK + (j + 1) * LANES] = kn.astype(BF16)
        va = _dot(h, wvg_ref[:, c * COL_CHUNK:(c + 1) * COL_CHUNK])
        v_ref[:, cols] = va.astype(BF16)
        ga = _dot(h, wvg_ref[:, SB_W + c * COL_CHUNK:SB_W + (c + 1) * COL_CHUNK])
        g_ref[:, cols] = _silu(ga)


def _attn_proj(x2, norm_g, w_in, layer, first_col, qn_g, kn_g):
    m, d = x2.shape
    row = lambda i: (i, 0)
    const = lambda i: (0, 0)
    wblk = 2 * SB_W
    assert first_col % wblk == 0
    wspec = lambda n: pl.BlockSpec((None, d, wblk), lambda i: (layer, 0, first_col // wblk + n),
                                   pipeline_mode=pl.Buffered(1))
    return pl.pallas_call(
        _attn_proj_kernel,
        grid=(m // ROW_TILE,),
        in_specs=[
            pl.BlockSpec((ROW_TILE, d), row),
            pl.BlockSpec((1, d), const),
            wspec(0),
            wspec(1),
            pl.BlockSpec((1, SB_HD), const),
            pl.BlockSpec((1, SB_HD), const),
        ],
        out_specs=[pl.BlockSpec((ROW_TILE, SB_W), row)] * 4,
        out_shape=[
            jax.ShapeDtypeStruct((m, SB_W), BF16),
            jax.ShapeDtypeStruct((m, SB_W), BF16),
            jax.ShapeDtypeStruct((m, SB_W), BF16),
            jax.ShapeDtypeStruct((m, SB_W), F32),
        ],
        compiler_params=pltpu.CompilerParams(
            dimension_semantics=("parallel",), vmem_limit_bytes=VMEM_LIMIT_BYTES),
        name="attn_proj",
    )(x2, norm_g, w_in, w_in, qn_g, kn_g)


def _conv_gla_proj_kernel(x_ref, ng_ref, wc_ref, wg_ref, cw_ref, cb_ref, wup_ref, bup_ref,
                          mixc_ref, glq_ref, glk_ref, glv_ref, gate_ref, gdec_ref, ubuf_ref,
                          *, conv_w):
    tm = x_ref.shape[0]
    halo = SUBLANES

    @pl.when(pl.program_id(1) == 0)
    def _():
        ubuf_ref[0:halo, :] = jnp.zeros((halo, conv_w), F32)

    h = _normed_input(x_ref, ng_ref)
    cvh = _dot(h, wc_ref[:, 0:conv_w])
    cvb = _dot(h, wc_ref[:, conv_w:2 * conv_w])
    cvc = _dot(h, wc_ref[:, 2 * conv_w:3 * conv_w])
    cvg = _dot(h, wc_ref[:, 3 * conv_w:4 * conv_w])
    u = cvc * cvh
    ubuf_ref[halo:halo + tm, :] = u
    um1 = ubuf_ref[halo - 1:halo - 1 + tm, :]
    um2 = ubuf_ref[halo - 2:halo - 2 + tm, :]
    y = cw_ref[0:1, :] * um2 + cw_ref[1:2, :] * um1 + cw_ref[2:3, :] * u + cb_ref[...]
    mixc_ref[...] = (cvb * y * _silu(cvg)).astype(BF16)
    ubuf_ref[0:halo, :] = u[tm - halo:tm, :]

    base = 0
    glq_ref[...] = _dot(h, wg_ref[:, base:base + GLA_KW])
    glk_ref[...] = _dot(h, wg_ref[:, base + GLA_KW:base + 2 * GLA_KW])
    base += 2 * GLA_KW
    glv_ref[...] = _dot(h, wg_ref[:, base:base + GLA_W]).astype(BF16)
    gate_ref[...] = _silu(_dot(h, wg_ref[:, base + GLA_W:base + 2 * GLA_W]))
    base += 2 * GLA_W
    lr = _dot(h, wg_ref[:, base:base + LANES])
    pre = _dot(lr.astype(BF16), wup_ref[...]) + bup_ref[...]
    log_sig = jnp.minimum(pre, 0.0) - jnp.log(1.0 + jnp.exp(-jnp.abs(pre)))
    gdec_ref[...] = log_sig / GLA_TAU


def _conv_gla_proj(x2, norm_g, w_in, w_gla, layer, conv_w, conv_b, wup, bup, *, batch, seq):
    m, d = x2.shape
    cw = conv_w.shape[1]
    n_tiles = seq // ROW_TILE
    row = lambda b, j: (b * n_tiles + j, 0)
    const = lambda b, j: (0, 0)
    wconst = lambda b, j: (layer, 0, 0)
    widths = [(cw, BF16), (GLA_KW, F32), (GLA_KW, F32), (GLA_W, BF16), (GLA_W, F32), (GLA_KW, F32)]
    return pl.pallas_call(
        functools.partial(_conv_gla_proj_kernel, conv_w=cw),
        grid=(batch, n_tiles),
        in_specs=[
            pl.BlockSpec((ROW_TILE, d), row),
            pl.BlockSpec((1, d), const),
            pl.BlockSpec((None, d, 4 * cw), wconst, pipeline_mode=pl.Buffered(1)),
            pl.BlockSpec((None, d, w_gla.shape[2]), wconst, pipeline_mode=pl.Buffered(1)),
            pl.BlockSpec((CONV_K, cw), const),
            pl.BlockSpec((1, cw), const),
            pl.BlockSpec((LANES, GLA_KW), const),
            pl.BlockSpec((1, GLA_KW), const),
        ],
        out_specs=[pl.BlockSpec((ROW_TILE, w), row) for w, _ in widths],
        out_shape=[jax.ShapeDtypeStruct((m, w), dt) for w, dt in widths],
        scratch_shapes=[pltpu.VMEM((ROW_TILE + SUBLANES, cw), F32)],
        compiler_params=pltpu.CompilerParams(
            dimension_semantics=("arbitrary", "arbitrary"), vmem_limit_bytes=VMEM_LIMIT_BYTES),
        name="conv_gla_proj",
    )(x2, norm_g, w_in, w_gla, conv_w, conv_b, wup, bup)


def _gla_tables():
    c, nl = GLA_CHUNK, GLA_LEVELS
    t = np.arange(c)
    nm = GLA_MATMUL_LEVELS
    sel = np.zeros((nm + 1, c, c), np.float32)
    msk = np.zeros((nl + 1, c, c), np.float32)
    for l in range(nl):
        up = ((t >> l) & 1) == 1
        lo_start = (t >> l) << l
        nxt = ((t >> l) + 1) << l
        for r in range(c if l < nm else 0):
            if up[r]:
                sel[l, r, lo_start[r]:r + 1] = 1.0
            else:
                sel[l, r, r + 1:nxt[r]] = 1.0
        same = (t[:, None] >> (l + 1)) == (t[None, :] >> (l + 1))
        msk[l] = (up[:, None] & ~up[None, :] & same).astype(np.float32)
    sel[nm] = (t[None, :] <= t[:, None]).astype(np.float32)
    msk[nl] = np.eye(c, dtype=np.float32)
    sel = sel.reshape((nm + 1) * c, c)
    msk = np.tile(msk, (1, 1, GLA_HEADS))
    head_of_k = np.arange(GLA_KW) // GLA_DK
    hmask = (head_of_k[None, :] == np.arange(GLA_HEADS)[:, None]).astype(np.float32)
    bd = (head_of_k[:, None] == (np.arange(GLA_W) // GLA_DV)[None, :]).astype(np.float32)
    return sel, msk, hmask.reshape(GLA_HEADS, 1, GLA_KW), bd


def _gla_kernel(q_ref, k_ref, v_ref, g_ref, gate_ref, ong_ref, sel_ref, msk_ref, hm_ref, bd_ref,
                o_ref, state_ref):
    c, nl, nm = GLA_CHUNK, GLA_LEVELS, GLA_MATMUL_LEVELS

    @pl.when(pl.program_id(1) == 0)
    def _():
        state_ref[...] = jnp.zeros_like(state_ref)

    sel = sel_ref[...]

    def stacked_k(kl):
        kb = kl.astype(BF16)
        return jnp.concatenate([kb * hm_ref[hh] for hh in range(GLA_HEADS)], axis=0)

    chunks = range(GLA_STEP_CHUNKS)
    rows = [slice(ci * c, (ci + 1) * c) for ci in chunks]
    e_alls = []
    for ci in chunks:
        g = g_ref[0, rows[ci], :]
        g1 = g.astype(BF16)
        g2 = (g - g1.astype(F32)).astype(BF16)
        e_alls.append(_dot(sel, g1) + _dot(sel, g2))

    def level_exponent(e_all, cum, l):
        if l < nm:
            return e_all[l * c:(l + 1) * c, :]
        slabs = []
        for r0 in range(0, c, SUBLANES):
            slab = cum[r0:r0 + SUBLANES, :]
            if (r0 >> l) & 1:
                first = (r0 >> l) << l
                slabs.append(slab - cum[first - 1:first, :])
            else:
                last_row = (((r0 >> l) + 1) << l) - 1
                slabs.append(cum[last_row:last_row + 1, :] - slab)
        return jnp.concatenate(slabs, axis=0)

    atts, q_decs, updates, decays = [], [], [], []
    for ci in chunks:
        e_all = e_alls[ci]
        cum = e_all[nm * c:(nm + 1) * c, :]
        q = q_ref[0, rows[ci], :] * (GLA_DK ** -0.5)
        k = k_ref[0, rows[ci], :]
        v = v_ref[0, rows[ci], :]
        att = msk_ref[nl] * _dot_nt(q.astype(BF16), stacked_k(k)).astype(BF16)
        for l in range(nl):
            e = jnp.exp(level_exponent(e_all, cum, l))
            att = att + msk_ref[l] * _dot_nt((q * e).astype(BF16), stacked_k(k * e)).astype(BF16)
        atts.append(att)
        q_decs.append((q * jnp.exp(cum)).astype(BF16))
        last = cum[c - 1:c, :]
        k_dec_t = (k * jnp.exp(last - cum)).T
        decay_t = jnp.broadcast_to(jnp.exp(last), (c, GLA_KW)).T
        decays.append(jnp.concatenate([decay_t] * (GLA_W // c), axis=1))
        updates.append(bd_ref[...] * _dot(k_dec_t.astype(BF16), v))

    state = state_ref[...]
    for ci in chunks:
        v = v_ref[0, rows[ci], :]
        o = _dot(q_decs[ci], state.astype(BF16))
        intra = [_dot(atts[ci][:, hh * c:(hh + 1) * c], v[:, hh * GLA_DV:(hh + 1) * GLA_DV])
                 for hh in range(GLA_HEADS)]
        o = o + jnp.concatenate(intra, axis=1)
        state = state * decays[ci] + updates[ci]
        normed = _head_rmsnorm(o, ong_ref[...])
        o_ref[0, rows[ci], :] = (jnp.concatenate(normed, axis=1) * gate_ref[0, rows[ci], :]).astype(BF16)
    state_ref[...] = state


def _gla(glq, glk, glv, gdec, gate, on_g):
    b, s, _ = glq.shape
    sel, msk, hm, bd = _gla_tables()
    c = GLA_CHUNK
    rows = c * GLA_STEP_CHUNKS
    blk = lambda w: pl.BlockSpec((1, rows, w), lambda bi, j: (bi, j, 0))
    const2 = lambda bi, j: (0, 0)
    const3 = lambda bi, j: (0, 0, 0)
    return pl.pallas_call(
        _gla_kernel,
        grid=(b, s // rows),
        in_specs=[
            blk(GLA_KW), blk(GLA_KW), blk(GLA_W), blk(GLA_KW), blk(GLA_W),
            pl.BlockSpec((1, GLA_DV), const2),
            pl.BlockSpec(sel.shape, const2),
            pl.BlockSpec(msk.shape, const3),
            pl.BlockSpec(hm.shape, const3),
            pl.BlockSpec(bd.shape, const2),
        ],
        out_specs=blk(GLA_W),
        out_shape=jax.ShapeDtypeStruct((b, s, GLA_W), BF16),
        scratch_shapes=[pltpu.VMEM((GLA_KW, GLA_W), F32)],
        compiler_params=pltpu.CompilerParams(
            dimension_semantics=("arbitrary", "arbitrary"), vmem_limit_bytes=VMEM_LIMIT_BYTES),
        name="gla",
    )(glq, glk, glv, gdec, gate, on_g, jnp.asarray(sel, BF16), jnp.asarray(msk, BF16), jnp.asarray(hm, BF16),
      jnp.asarray(bd))


def _sb_kernel(q_ref, k_ref, v_ref, gate_ref, tri_ref, o_ref):
    blk, nsub = SB_BLOCK, SB_SUBTILES
    i = pl.program_id(2)
    tri = tri_ref[...]
    qs = [q_ref[0, a * blk:(a + 1) * blk, :] for a in range(nsub)]
    diag = [i * nsub + a for a in range(nsub)]
    rows = lax.broadcasted_iota(jnp.int32, (blk, blk), 0)
    cols = lax.broadcasted_iota(jnp.int32, (blk, blk), 1)
    strictly_causal = cols < rows

    def key_rows(ref, kb):
        return ref[0, pl.ds(pl.multiple_of(kb * blk, blk), blk), :]

    def visit(blocks, carries, accs, on_diagonal):
        carries, accs = list(carries), list(accs)
        zs, sps, suffixes, ws = ([None] * nsub for _ in range(4))

        def scores(a):
            zs[a] = _dot_nt(qs[a], key_rows(k_ref, blocks[a]))

        def softplus2(a):
            z = zs[a]
            sp = jnp.maximum(z, 0.0) + jnp.log2(1.0 + jnp.exp2(jnp.minimum(z, -z)))
            if on_diagonal:
                sp = jnp.where(strictly_causal, sp, 0.0)
                zs[a] = jnp.where(strictly_causal, z, MASKED)
            sps[a] = sp.astype(BF16)

        def suffix_sum(a):
            suffixes[a] = _dot(sps[a], tri)

        def weights(a):
            d = jnp.minimum(zs[a] - suffixes[a], 0.0)
            carry = jnp.concatenate([carries[a]] * (blk // LANES), axis=1)
            ws[a] = jnp.exp2(d - carry).astype(BF16)
            carries[a] = carries[a] + jnp.broadcast_to(suffixes[a][:, 0:1], (blk, LANES))

        def accumulate(a):
            accs[a] = accs[a] + _dot(ws[a], key_rows(v_ref, blocks[a]))

        stages = (scores, softplus2, suffix_sum, weights, accumulate)
        for t in range(nsub + len(stages) - 1):
            for a in range(nsub):
                if 0 <= t - a < len(stages):
                    stages[t - a](a)
        return carries, accs

    def unfinished(carries, next_step):
        lowest = None
        for a in range(nsub):
            c = jnp.min(carries[a].reshape(blk // SUBLANES, SUBLANES, LANES), axis=0)
            c = jnp.where(diag[a] - next_step < 0, EXHAUSTED, c)
            lowest = c if lowest is None else jnp.minimum(lowest, c)
        return jnp.min(lowest) < F32_EXP2_UNDERFLOW

    carries = [jnp.zeros((blk, LANES), F32) for _ in range(nsub)]
    accs = [jnp.zeros((blk, SB_HD), F32) for _ in range(nsub)]
    carries, accs = visit(diag, carries, accs, on_diagonal=True)

    def body(state):
        _, step, carries, accs = state
        carries = [jnp.where(diag[a] - step < 0, EXHAUSTED, carries[a]) for a in range(nsub)]
        blocks = [jnp.maximum(diag[a] - step, 0) for a in range(nsub)]
        carries, accs = visit(blocks, list(carries), list(accs), on_diagonal=False)
        return unfinished(carries, step + 1), step + 1, tuple(carries), tuple(accs)

    state = (unfinished(carries, 1), jnp.int32(1), tuple(carries), tuple(accs))
    _, _, _, accs = lax.while_loop(lambda state: state[0], body, state)
    for a in range(nsub):
        rows_a = slice(a * blk, (a + 1) * blk)
        o_ref[0, rows_a, :] = (accs[a] * gate_ref[0, rows_a, :]).astype(BF16)


def _stick_breaking(q, k, v, gate):
    b, s, _ = q.shape
    blk = SB_BLOCK
    qblk = SB_BLOCK * SB_SUBTILES
    tri = (np.arange(blk)[:, None] >= np.arange(blk)[None, :]).astype(np.float32)
    qspec = pl.BlockSpec((1, qblk, SB_HD), lambda bi, h, i: (bi, i, h))
    kvspec = pl.BlockSpec((1, s, SB_HD), lambda bi, h, i: (bi, 0, h))
    return pl.pallas_call(
        _sb_kernel,
        grid=(b, SB_HEADS, s // qblk),
        in_specs=[qspec, kvspec, kvspec, qspec, pl.BlockSpec((blk, blk), lambda bi, h, i: (0, 0))],
        out_specs=qspec,
        out_shape=jax.ShapeDtypeStruct((b, s, SB_W), BF16),
        compiler_params=pltpu.CompilerParams(
            dimension_semantics=("parallel", "parallel", "arbitrary"),
            vmem_limit_bytes=VMEM_LIMIT_BYTES),
        name="stick_breaking",
    )(q, k, v, gate, jnp.asarray(tri, BF16))


def _out_proj_kernel(x_ref, mc_ref, ms_ref, mg_ref, w_ref, o_ref):
    cw = mc_ref.shape[1]
    y = _dot(mc_ref[...], w_ref[0:cw, :])
    y = y + _dot(ms_ref[...], w_ref[cw:cw + SB_W, :])
    y = y + _dot(mg_ref[...], w_ref[cw + SB_W:cw + SB_W + GLA_W, :])
    o_ref[...] = x_ref[...] + y


def _out_proj(x2, mixc, mixs, mixg, w_out, layer):
    m, d = x2.shape
    row = lambda i: (i, 0)
    return pl.pallas_call(
        _out_proj_kernel,
        grid=(m // ROW_TILE,),
        in_specs=[
            pl.BlockSpec((ROW_TILE, d), row),
            pl.BlockSpec((ROW_TILE, mixc.shape[1]), row),
            pl.BlockSpec((ROW_TILE, SB_W), row),
            pl.BlockSpec((ROW_TILE, GLA_W), row),
            pl.BlockSpec((None,) + w_out.shape[1:], lambda i: (layer, 0, 0), pipeline_mode=pl.Buffered(1)),
        ],
        out_specs=pl.BlockSpec((ROW_TILE, d), row),
        out_shape=jax.ShapeDtypeStruct((m, d), F32),
        compiler_params=pltpu.CompilerParams(
            dimension_semantics=("parallel",), vmem_limit_bytes=VMEM_LIMIT_BYTES),
        name="out_proj",
    )(x2, mixc, mixs, mixg, w_out)


def _layer(x2, batch, seq, layer, w_in, w_gla, w_out, norm_g, conv_w, conv_b, sb_qn_g, sb_kn_g,
           gla_w_up, gla_b_up, gla_on_g):
    pad = LANES - GLA_LR
    wup = jnp.concatenate([gla_w_up, jnp.zeros((pad, GLA_KW), gla_w_up.dtype)], axis=0).astype(BF16)
    row2 = lambda a: a.reshape(1, -1)
    q, k, v, sgate = _attn_proj(x2, row2(norm_g), w_in, layer, 4 * conv_w.shape[1],
                                row2(sb_qn_g), row2(sb_kn_g))
    mixc, glq, glk, glv, ggate, gdec = _conv_gla_proj(
        x2, row2(norm_g), w_in, w_gla, layer, conv_w, row2(conv_b), wup, row2(gla_b_up),
        batch=batch, seq=seq)

    r3 = lambda a: a.reshape(batch, seq, a.shape[-1])
    mixg = _gla(r3(glq), r3(glk), r3(glv), r3(gdec), r3(ggate), row2(gla_on_g))
    mixs = _stick_breaking(r3(q), r3(k), r3(v), r3(sgate))
    m = batch * seq
    return _out_proj(x2, mixc, mixs.reshape(m, SB_W), mixg.reshape(m, GLA_W), w_out, layer)


def kernel(x, norm_g, w_in, conv_w, conv_b, sb_qn_g, sb_kn_g, gla_w_up, gla_b_up, gla_on_g, w_out):
    batch, seq, d = x.shape
    assert seq % ROW_TILE == 0 and seq % (SB_BLOCK * SB_SUBTILES) == 0
    assert seq % (GLA_CHUNK * GLA_STEP_CHUNKS) == 0
    gla_first = 4 * conv_w.shape[2] + 4 * SB_W
    assert w_in.shape[2] - gla_first == 2 * GLA_KW + 2 * GLA_W + GLA_LR
    w_in_b = w_in.astype(BF16)
    w_gla_b = jnp.pad(w_in[:, :, gla_first:], ((0, 0), (0, 0), (0, LANES - GLA_LR))).astype(BF16)
    w_out_b = w_out.astype(BF16)
    h = x.reshape(batch * seq, d)
    for l in range(norm_g.shape[0]):
        h = _layer(h, batch, seq, l, w_in_b, w_gla_b, w_out_b, norm_g[l], conv_w[l], conv_b[l],
                   sb_qn_g[l], sb_kn_g[l], gla_w_up[l], gla_b_up[l], gla_on_g[l])
    return h.reshape(batch, seq, d)
```

```python
import functools

import numpy as np
import jax
import jax.numpy as jnp
from jax import lax
from jax.experimental import pallas as pl
from jax.experimental.pallas import tpu as pltpu

SB_HEADS = 8
SB_HD = 128
SB_W = SB_HEADS * SB_HD
GLA_HEADS = 4
GLA_DV = 128
GLA_DK = 64
GLA_W = GLA_HEADS * GLA_DV
GLA_KW = GLA_HEADS * GLA_DK
GLA_LR = 16
GLA_TAU = 16.0
CONV_K = 3
EPS = 1e-6

LANES = 128
SUBLANES = 8
VMEM_LIMIT_BYTES = 56 * 1024 * 1024

ROW_TILE = 512
COL_CHUNK = 512
GLA_CHUNK = 128
GLA_STEP_CHUNKS = 4
GLA_LEVELS = 7
GLA_MATMUL_LEVELS = 3
SB_BLOCK = 256
SB_SUBTILES = 8
SB_NEAR_ROWS = 160
MASKED = -1e30
EXHAUSTED = 1e30
F32_EXP2_UNDERFLOW = 150.0
LOG2_E = 1.4426950408889634

F32 = jnp.float32
BF16 = jnp.bfloat16


def _dot(a, b):
    return jnp.dot(a, b, preferred_element_type=F32)


def _dot_nt(a, b):
    return lax.dot_general(a, b, (((1,), (1,)), ((), ())), preferred_element_type=F32)


def _silu(a):
    return a / (1.0 + jnp.exp(-a))


def _normed_input(x_ref, ng_ref):
    xf = x_ref[...]
    ms = jnp.mean(xf * xf, axis=-1, keepdims=True)
    return (xf * lax.rsqrt(ms + EPS) * ng_ref[...]).astype(BF16)


def _head_rmsnorm(a, gain):
    outs = []
    for j in range(a.shape[1] // LANES):
        aj = a[:, j * LANES:(j + 1) * LANES]
        ms = jnp.mean(aj * aj, axis=-1, keepdims=True)
        outs.append(aj * lax.rsqrt(ms + EPS) * gain)
    return outs


def _attn_proj_kernel(x_ref, ng_ref, wqk_ref, wvg_ref, qg_ref, kg_ref, q_ref, k_ref, v_ref, g_ref):
    h = _normed_input(x_ref, ng_ref)
    q_scale = SB_HD ** -0.5 * LOG2_E
    n_chunks = SB_W // COL_CHUNK
    for c in range(n_chunks):
        cols = slice(c * COL_CHUNK, (c + 1) * COL_CHUNK)
        qa = _dot(h, wqk_ref[:, c * COL_CHUNK:(c + 1) * COL_CHUNK])
        for j, qn in enumerate(_head_rmsnorm(qa, qg_ref[...])):
            q_ref[:, c * COL_CHUNK + j * LANES:c * COL_CHUNK + (j + 1) * LANES] = (qn * q_scale).astype(BF16)
        ka = _dot(h, wqk_ref[:, SB_W + c * COL_CHUNK:SB_W + (c + 1) * COL_CHUNK])
        for j, kn in enumerate(_head_rmsnorm(ka, kg_ref[...])):
            k_ref[:, c * COL_CHUNK + j * LANES:c * COL_CHUNK + (j + 1) * LANES] = kn.astype(BF16)
        va = _dot(h, wvg_ref[:, c * COL_CHUNK:(c + 1) * COL_CHUNK])
        v_ref[:, cols] = va.astype(BF16)
        ga = _dot(h, wvg_ref[:, SB_W + c * COL_CHUNK:SB_W + (c + 1) * COL_CHUNK])
        g_ref[:, cols] = _silu(ga)


def _attn_proj(x2, norm_g, w_in, layer, first_col, qn_g, kn_g):
    m, d = x2.shape
    row = lambda i: (i, 0)
    const = lambda i: (0, 0)
    wblk = 2 * SB_W
    assert first_col % wblk == 0
    wspec = lambda n: pl.BlockSpec((None, d, wblk), lambda i: (layer, 0, first_col // wblk + n),
                                   pipeline_mode=pl.Buffered(1))
    return pl.pallas_call(
        _attn_proj_kernel,
        grid=(m // ROW_TILE,),
        in_specs=[
            pl.BlockSpec((ROW_TILE, d), row),
            pl.BlockSpec((1, d), const),
            wspec(0),
            wspec(1),
            pl.BlockSpec((1, SB_HD), const),
            pl.BlockSpec((1, SB_HD), const),
        ],
        out_specs=[pl.BlockSpec((ROW_TILE, SB_W), row)] * 4,
        out_shape=[
            jax.ShapeDtypeStruct((m, SB_W), BF16),
            jax.ShapeDtypeStruct((m, SB_W), BF16),
            jax.ShapeDtypeStruct((m, SB_W), BF16),
            jax.ShapeDtypeStruct((m, SB_W), F32),
        ],
        compiler_params=pltpu.CompilerParams(
            dimension_semantics=("parallel",), vmem_limit_bytes=VMEM_LIMIT_BYTES),
        name="attn_proj",
    )(x2, norm_g, w_in, w_in, qn_g, kn_g)


def _conv_gla_proj_kernel(x_ref, ng_ref, wc_ref, wg_ref, cw_ref, cb_ref, wup_ref, bup_ref,
                          mixc_ref, glq_ref, glk_ref, glv_ref, gate_ref, gdec_ref, ubuf_ref,
                          *, conv_w):
    tm = x_ref.shape[0]
    halo = SUBLANES

    @pl.when(pl.program_id(1) == 0)
    def _():
        ubuf_ref[0:halo, :] = jnp.zeros((halo, conv_w), F32)

    h = _normed_input(x_ref, ng_ref)
    cvh = _dot(h, wc_ref[:, 0:conv_w])
    cvb = _dot(h, wc_ref[:, conv_w:2 * conv_w])
    cvc = _dot(h, wc_ref[:, 2 * conv_w:3 * conv_w])
    cvg = _dot(h, wc_ref[:, 3 * conv_w:4 * conv_w])
    u = cvc * cvh
    ubuf_ref[halo:halo + tm, :] = u
    um1 = ubuf_ref[halo - 1:halo - 1 + tm, :]
    um2 = ubuf_ref[halo - 2:halo - 2 + tm, :]
    y = cw_ref[0:1, :] * um2 + cw_ref[1:2, :] * um1 + cw_ref[2:3, :] * u + cb_ref[...]
    mixc_ref[...] = (cvb * y * _silu(cvg)).astype(BF16)
    ubuf_ref[0:halo, :] = u[tm - halo:tm, :]

    base = 0
    glq_ref[...] = _dot(h, wg_ref[:, base:base + GLA_KW])
    glk_ref[...] = _dot(h, wg_ref[:, base + GLA_KW:base + 2 * GLA_KW])
    base += 2 * GLA_KW
    glv_ref[...] = _dot(h, wg_ref[:, base:base + GLA_W]).astype(BF16)
    gate_ref[...] = _silu(_dot(h, wg_ref[:, base + GLA_W:base + 2 * GLA_W]))
    base += 2 * GLA_W
    lr = _dot(h, wg_ref[:, base:base + LANES])
    pre = _dot(lr.astype(BF16), wup_ref[...]) + bup_ref[...]
    log_sig = jnp.minimum(pre, 0.0) - jnp.log(1.0 + jnp.exp(-jnp.abs(pre)))
    gdec_ref[...] = log_sig / GLA_TAU


def _conv_gla_proj(x2, norm_g, w_in, w_gla, layer, conv_w, conv_b, wup, bup, *, batch, seq):
    m, d = x2.shape
    cw = conv_w.shape[1]
    n_tiles = seq // ROW_TILE
    row = lambda b, j: (b * n_tiles + j, 0)
    const = lambda b, j: (0, 0)
    wconst = lambda b, j: (layer, 0, 0)
    widths = [(cw, BF16), (GLA_KW, F32), (GLA_KW, F32), (GLA_W, BF16), (GLA_W, F32), (GLA_KW, F32)]
    return pl.pallas_call(
        functools.partial(_conv_gla_proj_kernel, conv_w=cw),
        grid=(batch, n_tiles),
        in_specs=[
            pl.BlockSpec((ROW_TILE, d), row),
            pl.BlockSpec((1, d), const),
            pl.BlockSpec((None, d, 4 * cw), wconst, pipeline_mode=pl.Buffered(1)),
            pl.BlockSpec((None, d, w_gla.shape[2]), wconst, pipeline_mode=pl.Buffered(1)),
            pl.BlockSpec((CONV_K, cw), const),
            pl.BlockSpec((1, cw), const),
            pl.BlockSpec((LANES, GLA_KW), const),
            pl.BlockSpec((1, GLA_KW), const),
        ],
        out_specs=[pl.BlockSpec((ROW_TILE, w), row) for w, _ in widths],
        out_shape=[jax.ShapeDtypeStruct((m, w), dt) for w, dt in widths],
        scratch_shapes=[pltpu.VMEM((ROW_TILE + SUBLANES, cw), F32)],
        compiler_params=pltpu.CompilerParams(
            dimension_semantics=("arbitrary", "arbitrary"), vmem_limit_bytes=VMEM_LIMIT_BYTES),
        name="conv_gla_proj",
    )(x2, norm_g, w_in, w_gla, conv_w, conv_b, wup, bup)


def _gla_tables():
    c, nl = GLA_CHUNK, GLA_LEVELS
    t = np.arange(c)
    nm = GLA_MATMUL_LEVELS
    sel = np.zeros((nm + 1, c, c), np.float32)
    msk = np.zeros((nl + 1, c, c), np.float32)
    for l in range(nl):
        up = ((t >> l) & 1) == 1
        lo_start = (t >> l) << l
        nxt = ((t >> l) + 1) << l
        for r in range(c if l < nm else 0):
            if up[r]:
                sel[l, r, lo_start[r]:r + 1] = 1.0
            else:
                sel[l, r, r + 1:nxt[r]] = 1.0
        same = (t[:, None] >> (l + 1)) == (t[None, :] >> (l + 1))
        msk[l] = (up[:, None] & ~up[None, :] & same).astype(np.float32)
    sel[nm] = (t[None, :] <= t[:, None]).astype(np.float32)
    msk[nl] = np.eye(c, dtype=np.float32)
    sel = sel.reshape((nm + 1) * c, c)
    msk = np.tile(msk, (1, 1, GLA_HEADS))
    head_of_k = np.arange(GLA_KW) // GLA_DK
    hmask = (head_of_k[None, :] == np.arange(GLA_HEADS)[:, None]).astype(np.float32)
    bd = (head_of_k[:, None] == (np.arange(GLA_W) // GLA_DV)[None, :]).astype(np.float32)
    return sel, msk, hmask.reshape(GLA_HEADS, 1, GLA_KW), bd


def _gla_kernel(q_ref, k_ref, v_ref, g_ref, gate_ref, ong_ref, sel_ref, msk_ref, hm_ref, bd_ref,
                o_ref, state_ref):
    c, nl, nm = GLA_CHUNK, GLA_LEVELS, GLA_MATMUL_LEVELS

    @pl.when(pl.program_id(1) == 0)
    def _():
        state_ref[...] = jnp.zeros_like(state_ref)

    sel = sel_ref[...]

    def stacked_k(kl):
        kb = kl.astype(BF16)
        return jnp.concatenate([kb * hm_ref[hh] for hh in range(GLA_HEADS)], axis=0)

    chunks = range(GLA_STEP_CHUNKS)
    rows = [slice(ci * c, (ci + 1) * c) for ci in chunks]
    e_alls = []
    for ci in chunks:
        g = g_ref[0, rows[ci], :]
        g1 = g.astype(BF16)
        g2 = (g - g1.astype(F32)).astype(BF16)
        e_alls.append(_dot(sel, g1) + _dot(sel, g2))

    def level_exponent(e_all, cum, l):
        if l < nm:
            return e_all[l * c:(l + 1) * c, :]
        slabs = []
        for r0 in range(0, c, SUBLANES):
            slab = cum[r0:r0 + SUBLANES, :]
            if (r0 >> l) & 1:
                first = (r0 >> l) << l
                slabs.append(slab - cum[first - 1:first, :])
            else:
                last_row = (((r0 >> l) + 1) << l) - 1
                slabs.append(cum[last_row:last_row + 1, :] - slab)
        return jnp.concatenate(slabs, axis=0)

    atts, q_decs, updates, decays = [], [], [], []
    for ci in chunks:
        e_all = e_alls[ci]
        cum = e_all[nm * c:(nm + 1) * c, :]
        q = q_ref[0, rows[ci], :] * (GLA_DK ** -0.5)
        k = k_ref[0, rows[ci], :]
        v = v_ref[0, rows[ci], :]
        att = msk_ref[nl] * _dot_nt(q.astype(BF16), stacked_k(k)).astype(BF16)
        for l in range(nl):
            e = jnp.exp(level_exponent(e_all, cum, l))
            att = att + msk_ref[l] * _dot_nt((q * e).astype(BF16), stacked_k(k * e)).astype(BF16)
        atts.append(att)
        q_decs.append((q * jnp.exp(cum)).astype(BF16))
        last = cum[c - 1:c, :]
        k_dec_t = (k * jnp.exp(last - cum)).T
        decay_t = jnp.broadcast_to(jnp.exp(last), (c, GLA_KW)).T
        decays.append(jnp.concatenate([decay_t] * (GLA_W // c), axis=1))
        updates.append(bd_ref[...] * _dot(k_dec_t.astype(BF16), v))

    state = state_ref[...]
    for ci in chunks:
        v = v_ref[0, rows[ci], :]
        o = _dot(q_decs[ci], state.astype(BF16))
        intra = [_dot(atts[ci][:, hh * c:(hh + 1) * c], v[:, hh * GLA_DV:(hh + 1) * GLA_DV])
                 for hh in range(GLA_HEADS)]
        o = o + jnp.concatenate(intra, axis=1)
        state = state * decays[ci] + updates[ci]
        normed = _head_rmsnorm(o, ong_ref[...])
        o_ref[0, rows[ci], :] = (jnp.concatenate(normed, axis=1) * gate_ref[0, rows[ci], :]).astype(BF16)
    state_ref[...] = state


def _gla(glq, glk, glv, gdec, gate, on_g):
    b, s, _ = glq.shape
    sel, msk, hm, bd = _gla_tables()
    c = GLA_CHUNK
    rows = c * GLA_STEP_CHUNKS
    blk = lambda w: pl.BlockSpec((1, rows, w), lambda bi, j: (bi, j, 0))
    const2 = lambda bi, j: (0, 0)
    const3 = lambda bi, j: (0, 0, 0)
    return pl.pallas_call(
        _gla_kernel,
        grid=(b, s // rows),
        in_specs=[
            blk(GLA_KW), blk(GLA_KW), blk(GLA_W), blk(GLA_KW), blk(GLA_W),
            pl.BlockSpec((1, GLA_DV), const2),
            pl.BlockSpec(sel.shape, const2),
            pl.BlockSpec(msk.shape, const3),
            pl.BlockSpec(hm.shape, const3),
            pl.BlockSpec(bd.shape, const2),
        ],
        out_specs=blk(GLA_W),
        out_shape=jax.ShapeDtypeStruct((b, s, GLA_W), BF16),
        scratch_shapes=[pltpu.VMEM((GLA_KW, GLA_W), F32)],
        compiler_params=pltpu.CompilerParams(
            dimension_semantics=("arbitrary", "arbitrary"), vmem_limit_bytes=VMEM_LIMIT_BYTES),
        name="gla",
    )(glq, glk, glv, gdec, gate, on_g, jnp.asarray(sel, BF16), jnp.asarray(msk, BF16), jnp.asarray(hm, BF16),
      jnp.asarray(bd))


def _sb_kernel(q_ref, k_ref, v_ref, gate_ref, tri_ref, o_ref):
    blk, nsub = SB_BLOCK, SB_SUBTILES
    i = pl.program_id(2)
    tri = tri_ref[...]
    diag = [i * nsub + a for a in range(nsub)]
    strictly_causal = (lax.broadcasted_iota(jnp.int32, (blk, blk), 1)
                       < lax.broadcasted_iota(jnp.int32, (blk, blk), 0))

    def key_rows(ref, kb):
        return ref[0, pl.ds(pl.multiple_of(kb * blk, blk), blk), :]

    def sub_rows(a, part):
        return slice(a * blk + part.start, a * blk + part.stop)

    def visit(blocks, part, carries, accs, on_diagonal):
        n_rows = part.stop - part.start
        carries, accs = list(carries), list(accs)
        zs, sps, suffixes, ws = ([None] * nsub for _ in range(4))

        def scores(a):
            zs[a] = _dot_nt(q_ref[0, sub_rows(a, part), :], key_rows(k_ref, blocks[a]))

        def softplus2(a):
            z = zs[a]
            sp = jnp.maximum(z, 0.0) + jnp.log2(1.0 + jnp.exp2(jnp.minimum(z, -z)))
            if on_diagonal:
                visible = strictly_causal[part, :]
                sp = jnp.where(visible, sp, 0.0)
                zs[a] = jnp.where(visible, z, MASKED)
            sps[a] = sp.astype(BF16)

        def suffix_sum(a):
            suffixes[a] = _dot(sps[a], tri)

        def weights(a):
            d = jnp.minimum(zs[a] - suffixes[a], 0.0)
            carry = jnp.concatenate([carries[a]] * (blk // LANES), axis=1)
            ws[a] = jnp.exp2(d - carry).astype(BF16)
            carries[a] = carries[a] + jnp.broadcast_to(suffixes[a][:, 0:1], (n_rows, LANES))

        def accumulate(a):
            accs[a] = accs[a] + _dot(ws[a], key_rows(v_ref, blocks[a]))

        stages = (scores, softplus2, suffix_sum, weights, accumulate)
        for t in range(nsub + len(stages) - 1):
            for a in range(nsub):
                if 0 <= t - a < len(stages):
                    stages[t - a](a)
        return carries, accs

    def unfinished(carries, next_step):
        lowest = None
        for a in range(nsub):
            c = jnp.min(carries[a].reshape(-1, SUBLANES, LANES), axis=0)
            c = jnp.where(diag[a] - next_step < 0, EXHAUSTED, c)
            lowest = c if lowest is None else jnp.minimum(lowest, c)
        return jnp.min(lowest) < F32_EXP2_UNDERFLOW

    def walk_back(part, carries, accs):
        def body(state):
            _, step, carries, accs = state
            carries = [jnp.where(diag[a] - step < 0, EXHAUSTED, carries[a]) for a in range(nsub)]
            blocks = [jnp.maximum(diag[a] - step, 0) for a in range(nsub)]
            carries, accs = visit(blocks, part, carries, accs, on_diagonal=False)
            return unfinished(carries, step + 1), step + 1, tuple(carries), tuple(accs)

        state = (unfinished(carries, 1), jnp.int32(1), tuple(carries), tuple(accs))
        return lax.while_loop(lambda state: state[0], body, state)[3]

    whole = slice(0, blk)
    carries = [jnp.zeros((blk, LANES), F32) for _ in range(nsub)]
    accs = [jnp.zeros((blk, SB_HD), F32) for _ in range(nsub)]
    carries, accs = visit(diag, whole, carries, accs, on_diagonal=True)
    for part in (slice(0, SB_NEAR_ROWS), slice(SB_NEAR_ROWS, blk)):
        part_accs = walk_back(part, [c[part] for c in carries], [x[part] for x in accs])
        for a in range(nsub):
            rows_a = sub_rows(a, part)
            o_ref[0, rows_a, :] = (part_accs[a] * gate_ref[0, rows_a, :]).astype(BF16)


def _stick_breaking(q, k, v, gate):
    b, s, _ = q.shape
    blk = SB_BLOCK
    qblk = SB_BLOCK * SB_SUBTILES
    tri = (np.arange(blk)[:, None] >= np.arange(blk)[None, :]).astype(np.float32)
    qspec = pl.BlockSpec((1, qblk, SB_HD), lambda bi, h, i: (bi, i, h))
    kvspec = pl.BlockSpec((1, s, SB_HD), lambda bi, h, i: (bi, 0, h))
    return pl.pallas_call(
        _sb_kernel,
        grid=(b, SB_HEADS, s // qblk),
        in_specs=[qspec, kvspec, kvspec, qspec, pl.BlockSpec((blk, blk), lambda bi, h, i: (0, 0))],
        out_specs=qspec,
        out_shape=jax.ShapeDtypeStruct((b, s, SB_W), BF16),
        compiler_params=pltpu.CompilerParams(
            dimension_semantics=("parallel", "parallel", "arbitrary"),
            vmem_limit_bytes=VMEM_LIMIT_BYTES),
        name="stick_breaking",
    )(q, k, v, gate, jnp.asarray(tri, BF16))


def _out_proj_kernel(x_ref, mc_ref, ms_ref, mg_ref, w_ref, o_ref):
    cw = mc_ref.shape[1]
    y = _dot(mc_ref[...], w_ref[0:cw, :])
    y = y + _dot(ms_ref[...], w_ref[cw:cw + SB_W, :])
    y = y + _dot(mg_ref[...], w_ref[cw + SB_W:cw + SB_W + GLA_W, :])
    o_ref[...] = x_ref[...] + y


def _out_proj(x2, mixc, mixs, mixg, w_out, layer):
    m, d = x2.shape
    row = lambda i: (i, 0)
    return pl.pallas_call(
        _out_proj_kernel,
        grid=(m // ROW_TILE,),
        in_specs=[
            pl.BlockSpec((ROW_TILE, d), row),
            pl.BlockSpec((ROW_TILE, mixc.shape[1]), row),
            pl.BlockSpec((ROW_TILE, SB_W), row),
            pl.BlockSpec((ROW_TILE, GLA_W), row),
            pl.BlockSpec((None,) + w_out.shape[1:], lambda i: (layer, 0, 0), pipeline_mode=pl.Buffered(1)),
        ],
        out_specs=pl.BlockSpec((ROW_TILE, d), row),
        out_shape=jax.ShapeDtypeStruct((m, d), F32),
        compiler_params=pltpu.CompilerParams(
            dimension_semantics=("parallel",), vmem_limit_bytes=VMEM_LIMIT_BYTES),
        name="out_proj",
    )(x2, mixc, mixs, mixg, w_out)


def _layer(x2, batch, seq, layer, w_in, w_gla, w_out, norm_g, conv_w, conv_b, sb_qn_g, sb_kn_g,
           gla_w_up, gla_b_up, gla_on_g):
    pad = LANES - GLA_LR
    wup = jnp.concatenate([gla_w_up, jnp.zeros((pad, GLA_KW), gla_w_up.dtype)], axis=0).astype(BF16)
    row2 = lambda a: a.reshape(1, -1)
    q, k, v, sgate = _attn_proj(x2, row2(norm_g), w_in, layer, 4 * conv_w.shape[1],
                                row2(sb_qn_g), row2(sb_kn_g))
    mixc, glq, glk, glv, ggate, gdec = _conv_gla_proj(
        x2, row2(norm_g), w_in, w_gla, layer, conv_w, row2(conv_b), wup, row2(gla_b_up),
        batch=batch, seq=seq)

    r3 = lambda a: a.reshape(batch, seq, a.shape[-1])
    mixg = _gla(r3(glq), r3(glk), r3(glv), r3(gdec), r3(ggate), row2(gla_on_g))
    mixs = _stick_breaking(r3(q), r3(k), r3(v), r3(sgate))
    m = batch * seq
    return _out_proj(x2, mixc, mixs.reshape(m, SB_W), mixg.reshape(m, GLA_W), w_out, layer)


def kernel(x, norm_g, w_in, conv_w, conv_b, sb_qn_g, sb_kn_g, gla_w_up, gla_b_up, gla_on_g, w_out):
    batch, seq, d = x.shape
    assert seq % ROW_TILE == 0 and seq % (SB_BLOCK * SB_SUBTILES) == 0
    assert seq % (GLA_CHUNK * GLA_STEP_CHUNKS) == 0
    gla_first = 4 * conv_w.shape[2] + 4 * SB_W
    assert w_in.shape[2] - gla_first == 2 * GLA_KW + 2 * GLA_W + GLA_LR
    w_in_b = w_in.astype(BF16)
    w_gla_b = jnp.pad(w_in[:, :, gla_first:], ((0, 0), (0, 0), (0, LANES - GLA_LR))).astype(BF16)
    w_out_b = w_out.astype(BF16)
    h = x.reshape(batch * seq, d)
    for l in range(norm_g.shape[0]):
        h = _layer(h, batch, seq, l, w_in_b, w_gla_b, w_out_b, norm_g[l], conv_w[l], conv_b[l],
                   sb_qn_g[l], sb_kn_g[l], gla_w_up[l], gla_b_up[l], gla_on_g[l])
    return h.reshape(batch, seq, d)
```

```python
import functools

import numpy as np
import jax
import jax.numpy as jnp
from jax import lax
from jax.experimental import pallas as pl
from jax.experimental.pallas import tpu as pltpu

SB_HEADS = 8
SB_HD = 128
SB_W = SB_HEADS * SB_HD
GLA_HEADS = 4
GLA_DV = 128
GLA_DK = 64
GLA_W = GLA_HEADS * GLA_DV
GLA_KW = GLA_HEADS * GLA_DK
GLA_LR = 16
GLA_TAU = 16.0
CONV_K = 3
EPS = 1e-6

LANES = 128
SUBLANES = 8
VMEM_LIMIT_BYTES = 56 * 1024 * 1024

ROW_TILE = 512
COL_CHUNK = 512
GLA_CHUNK = 128
GLA_STEP_CHUNKS = 4
GLA_LEVELS = 7
GLA_MATMUL_LEVELS = 3
SB_BLOCK = 256
SB_SUBTILES = 8
SB_NEAR_ROWS = 160
MASKED = -1e30
EXHAUSTED = 1e30
F32_EXP2_UNDERFLOW = 150.0
LOG2_E = 1.4426950408889634

F32 = jnp.float32
BF16 = jnp.bfloat16


def _dot(a, b):
    return jnp.dot(a, b, preferred_element_type=F32)


def _dot_nt(a, b):
    return lax.dot_general(a, b, (((1,), (1,)), ((), ())), preferred_element_type=F32)


def _silu(a):
    return a / (1.0 + jnp.exp(-a))


def _normed_input(x_ref, ng_ref):
    xf = x_ref[...]
    ms = jnp.mean(xf * xf, axis=-1, keepdims=True)
    return (xf * lax.rsqrt(ms + EPS) * ng_ref[...]).astype(BF16)


def _head_rmsnorm(a, gain):
    outs = []
    for j in range(a.shape[1] // LANES):
        aj = a[:, j * LANES:(j + 1) * LANES]
        ms = jnp.mean(aj * aj, axis=-1, keepdims=True)
        outs.append(aj * lax.rsqrt(ms + EPS) * gain)
    return outs


def _attn_proj_kernel(x_ref, ng_ref, wqk_ref, wvg_ref, qg_ref, kg_ref, q_ref, k_ref, v_ref, g_ref):
    h = _normed_input(x_ref, ng_ref)
    q_scale = SB_HD ** -0.5 * LOG2_E
    n_chunks = SB_W // COL_CHUNK
    for c in range(n_chunks):
        cols = slice(c * COL_CHUNK, (c + 1) * COL_CHUNK)
        qa = _dot(h, wqk_ref[:, c * COL_CHUNK:(c + 1) * COL_CHUNK].astype(BF16))
        for j, qn in enumerate(_head_rmsnorm(qa, qg_ref[...])):
            q_ref[:, c * COL_CHUNK + j * LANES:c * COL_CHUNK + (j + 1) * LANES] = (qn * q_scale).astype(BF16)
        ka = _dot(h, wqk_ref[:, SB_W + c * COL_CHUNK:SB_W + (c + 1) * COL_CHUNK].astype(BF16))
        for j, kn in enumerate(_head_rmsnorm(ka, kg_ref[...])):
            k_ref[:, c * COL_CHUNK + j * LANES:c * COL_CHUNK + (j + 1) * LANES] = kn.astype(BF16)
        va = _dot(h, wvg_ref[:, c * COL_CHUNK:(c + 1) * COL_CHUNK].astype(BF16))
        v_ref[:, cols] = va.astype(BF16)
        ga = _dot(h, wvg_ref[:, SB_W + c * COL_CHUNK:SB_W + (c + 1) * COL_CHUNK].astype(BF16))
        g_ref[:, cols] = _silu(ga)


def _attn_proj(x2, norm_g, w_in, layer, first_col, qn_g, kn_g):
    m, d = x2.shape
    row = lambda i: (i, 0)
    const = lambda i: (0, 0)
    wblk = 2 * SB_W
    assert first_col % wblk == 0
    wspec = lambda n: pl.BlockSpec((None, d, wblk), lambda i: (layer, 0, first_col // wblk + n),
                                   pipeline_mode=pl.Buffered(1))
    return pl.pallas_call(
        _attn_proj_kernel,
        grid=(m // ROW_TILE,),
        in_specs=[
            pl.BlockSpec((ROW_TILE, d), row),
            pl.BlockSpec((1, d), const),
            wspec(0),
            wspec(1),
            pl.BlockSpec((1, SB_HD), const),
            pl.BlockSpec((1, SB_HD), const),
        ],
        out_specs=[pl.BlockSpec((ROW_TILE, SB_W), row)] * 4,
        out_shape=[
            jax.ShapeDtypeStruct((m, SB_W), BF16),
            jax.ShapeDtypeStruct((m, SB_W), BF16),
            jax.ShapeDtypeStruct((m, SB_W), BF16),
            jax.ShapeDtypeStruct((m, SB_W), F32),
        ],
        compiler_params=pltpu.CompilerParams(
            dimension_semantics=("parallel",), vmem_limit_bytes=VMEM_LIMIT_BYTES),
        name="attn_proj",
    )(x2, norm_g, w_in, w_in, qn_g, kn_g)


def _conv_gla_proj_kernel(x_ref, ng_ref, wc_ref, wg_ref, wlr_ref, cw_ref, cb_ref, wup_ref, bup_ref,
                          mixc_ref, glq_ref, glk_ref, glv_ref, gate_ref, gdec_ref, ubuf_ref,
                          *, conv_w):
    tm = x_ref.shape[0]
    halo = SUBLANES

    @pl.when(pl.program_id(1) == 0)
    def _():
        ubuf_ref[0:halo, :] = jnp.zeros((halo, conv_w), F32)

    h = _normed_input(x_ref, ng_ref)
    proj = lambda w_ref, first, width: _dot(h, w_ref[:, first:first + width].astype(BF16))
    cvh = proj(wc_ref, 0, conv_w)
    cvb = proj(wc_ref, conv_w, conv_w)
    cvc = proj(wc_ref, 2 * conv_w, conv_w)
    cvg = proj(wc_ref, 3 * conv_w, conv_w)
    u = cvc * cvh
    ubuf_ref[halo:halo + tm, :] = u
    um1 = ubuf_ref[halo - 1:halo - 1 + tm, :]
    um2 = ubuf_ref[halo - 2:halo - 2 + tm, :]
    y = cw_ref[0:1, :] * um2 + cw_ref[1:2, :] * um1 + cw_ref[2:3, :] * u + cb_ref[...]
    mixc_ref[...] = (cvb * y * _silu(cvg)).astype(BF16)
    ubuf_ref[0:halo, :] = u[tm - halo:tm, :]

    glq_ref[...] = proj(wg_ref, 0, GLA_KW)
    glk_ref[...] = proj(wg_ref, GLA_KW, GLA_KW)
    glv_ref[...] = proj(wg_ref, 2 * GLA_KW, GLA_W).astype(BF16)
    gate_ref[...] = _silu(proj(wg_ref, 2 * GLA_KW + GLA_W, GLA_W))
    lr = proj(wlr_ref, 0, LANES)
    pre = _dot(lr.astype(BF16), wup_ref[...]) + bup_ref[...]
    log_sig = jnp.minimum(pre, 0.0) - jnp.log(1.0 + jnp.exp(-jnp.abs(pre)))
    gdec_ref[...] = log_sig / GLA_TAU


def _conv_gla_proj(x2, norm_g, w_in, gla_first, w_lr, layer, conv_w, conv_b, wup, bup, *, batch, seq):
    m, d = x2.shape
    cw = conv_w.shape[1]
    n_tiles = seq // ROW_TILE
    row = lambda b, j: (b * n_tiles + j, 0)
    const = lambda b, j: (0, 0)
    wconst = lambda b, j: (layer, 0, 0)
    gla_cols = 2 * GLA_KW + 2 * GLA_W
    assert gla_first % gla_cols == 0
    widths = [(cw, BF16), (GLA_KW, F32), (GLA_KW, F32), (GLA_W, BF16), (GLA_W, F32), (GLA_KW, F32)]
    return pl.pallas_call(
        functools.partial(_conv_gla_proj_kernel, conv_w=cw),
        grid=(batch, n_tiles),
        in_specs=[
            pl.BlockSpec((ROW_TILE, d), row),
            pl.BlockSpec((1, d), const),
            pl.BlockSpec((None, d, 4 * cw), wconst, pipeline_mode=pl.Buffered(1)),
            pl.BlockSpec((None, d, gla_cols), lambda b, j: (layer, 0, gla_first // gla_cols),
                         pipeline_mode=pl.Buffered(1)),
            pl.BlockSpec((None, d, LANES), wconst),
            pl.BlockSpec((CONV_K, cw), const),
            pl.BlockSpec((1, cw), const),
            pl.BlockSpec((LANES, GLA_KW), const),
            pl.BlockSpec((1, GLA_KW), const),
        ],
        out_specs=[pl.BlockSpec((ROW_TILE, w), row) for w, _ in widths],
        out_shape=[jax.ShapeDtypeStruct((m, w), dt) for w, dt in widths],
        scratch_shapes=[pltpu.VMEM((ROW_TILE + SUBLANES, cw), F32)],
        compiler_params=pltpu.CompilerParams(
            dimension_semantics=("arbitrary", "arbitrary"), vmem_limit_bytes=VMEM_LIMIT_BYTES),
        name="conv_gla_proj",
    )(x2, norm_g, w_in, w_in, w_lr, conv_w, conv_b, wup, bup)


def _gla_tables():
    c, nl = GLA_CHUNK, GLA_LEVELS
    t = np.arange(c)
    nm = GLA_MATMUL_LEVELS
    sel = np.zeros((nm + 1, c, c), np.float32)
    msk = np.zeros((nl + 1, c, c), np.float32)
    for l in range(nl):
        up = ((t >> l) & 1) == 1
        lo_start = (t >> l) << l
        nxt = ((t >> l) + 1) << l
        for r in range(c if l < nm else 0):
            if up[r]:
                sel[l, r, lo_start[r]:r + 1] = 1.0
            else:
                sel[l, r, r + 1:nxt[r]] = 1.0
        same = (t[:, None] >> (l + 1)) == (t[None, :] >> (l + 1))
        msk[l] = (up[:, None] & ~up[None, :] & same).astype(np.float32)
    sel[nm] = (t[None, :] <= t[:, None]).astype(np.float32)
    msk[nl] = np.eye(c, dtype=np.float32)
    sel = sel.reshape((nm + 1) * c, c)
    msk = np.tile(msk, (1, 1, GLA_HEADS))
    head_of_k = np.arange(GLA_KW) // GLA_DK
    hmask = (head_of_k[None, :] == np.arange(GLA_HEADS)[:, None]).astype(np.float32)
    bd = (head_of_k[:, None] == (np.arange(GLA_W) // GLA_DV)[None, :]).astype(np.float32)
    return sel, msk, hmask.reshape(GLA_HEADS, 1, GLA_KW), bd


def _gla_kernel(q_ref, k_ref, v_ref, g_ref, gate_ref, ong_ref, sel_ref, msk_ref, hm_ref, bd_ref,
                o_ref, state_ref):
    c, nl, nm = GLA_CHUNK, GLA_LEVELS, GLA_MATMUL_LEVELS

    @pl.when(pl.program_id(1) == 0)
    def _():
        state_ref[...] = jnp.zeros_like(state_ref)

    sel = sel_ref[...]

    def stacked_k(kl):
        kb = kl.astype(BF16)
        return jnp.concatenate([kb * hm_ref[hh] for hh in range(GLA_HEADS)], axis=0)

    chunks = range(GLA_STEP_CHUNKS)
    rows = [slice(ci * c, (ci + 1) * c) for ci in chunks]
    e_alls = []
    for ci in chunks:
        g = g_ref[0, rows[ci], :]
        g1 = g.astype(BF16)
        g2 = (g - g1.astype(F32)).astype(BF16)
        e_alls.append(_dot(sel, g1) + _dot(sel, g2))

    def level_exponent(e_all, cum, l):
        if l < nm:
            return e_all[l * c:(l + 1) * c, :]
        slabs = []
        for r0 in range(0, c, SUBLANES):
            slab = cum[r0:r0 + SUBLANES, :]
            if (r0 >> l) & 1:
                first = (r0 >> l) << l
                slabs.append(slab - cum[first - 1:first, :])
            else:
                last_row = (((r0 >> l) + 1) << l) - 1
                slabs.append(cum[last_row:last_row + 1, :] - slab)
        return jnp.concatenate(slabs, axis=0)

    atts, q_decs, updates, decays = [], [], [], []
    for ci in chunks:
        e_all = e_alls[ci]
        cum = e_all[nm * c:(nm + 1) * c, :]
        q = q_ref[0, rows[ci], :] * (GLA_DK ** -0.5)
        k = k_ref[0, rows[ci], :]
        v = v_ref[0, rows[ci], :]
        att = msk_ref[nl] * _dot_nt(q.astype(BF16), stacked_k(k)).astype(BF16)
        for l in range(nl):
            e = jnp.exp(level_exponent(e_all, cum, l))
            att = att + msk_ref[l] * _dot_nt((q * e).astype(BF16), stacked_k(k * e)).astype(BF16)
        atts.append(att)
        q_decs.append((q * jnp.exp(cum)).astype(BF16))
        last = cum[c - 1:c, :]
        k_dec_t = (k * jnp.exp(last - cum)).T
        decay_t = jnp.broadcast_to(jnp.exp(last), (c, GLA_KW)).T
        decays.append(jnp.concatenate([decay_t] * (GLA_W // c), axis=1))
        updates.append(bd_ref[...] * _dot(k_dec_t.astype(BF16), v))

    state = state_ref[...]
    for ci in chunks:
        v = v_ref[0, rows[ci], :]
        o = _dot(q_decs[ci], state.astype(BF16))
        intra = [_dot(atts[ci][:, hh * c:(hh + 1) * c], v[:, hh * GLA_DV:(hh + 1) * GLA_DV])
                 for hh in range(GLA_HEADS)]
        o = o + jnp.concatenate(intra, axis=1)
        state = state * decays[ci] + updates[ci]
        normed = _head_rmsnorm(o, ong_ref[...])
        o_ref[0, rows[ci], :] = (jnp.concatenate(normed, axis=1) * gate_ref[0, rows[ci], :]).astype(BF16)
    state_ref[...] = state


def _gla(glq, glk, glv, gdec, gate, on_g):
    b, s, _ = glq.shape
    sel, msk, hm, bd = _gla_tables()
    c = GLA_CHUNK
    rows = c * GLA_STEP_CHUNKS
    blk = lambda w: pl.BlockSpec((1, rows, w), lambda bi, j: (bi, j, 0))
    const2 = lambda bi, j: (0, 0)
    const3 = lambda bi, j: (0, 0, 0)
    return pl.pallas_call(
        _gla_kernel,
        grid=(b, s // rows),
        in_specs=[
            blk(GLA_KW), blk(GLA_KW), blk(GLA_W), blk(GLA_KW), blk(GLA_W),
            pl.BlockSpec((1, GLA_DV), const2),
            pl.BlockSpec(sel.shape, const2),
            pl.BlockSpec(msk.shape, const3),
            pl.BlockSpec(hm.shape, const3),
            pl.BlockSpec(bd.shape, const2),
        ],
        out_specs=blk(GLA_W),
        out_shape=jax.ShapeDtypeStruct((b, s, GLA_W), BF16),
        scratch_shapes=[pltpu.VMEM((GLA_KW, GLA_W), F32)],
        compiler_params=pltpu.CompilerParams(
            dimension_semantics=("arbitrary", "arbitrary"), vmem_limit_bytes=VMEM_LIMIT_BYTES),
        name="gla",
    )(glq, glk, glv, gdec, gate, on_g, jnp.asarray(sel, BF16), jnp.asarray(msk, BF16), jnp.asarray(hm, BF16),
      jnp.asarray(bd))


def _sb_kernel(q_ref, k_ref, v_ref, gate_ref, tri_ref, o_ref):
    blk, nsub = SB_BLOCK, SB_SUBTILES
    i = pl.program_id(2)
    tri = tri_ref[...]
    diag = [i * nsub + a for a in range(nsub)]
    strictly_causal = (lax.broadcasted_iota(jnp.int32, (blk, blk), 1)
                       < lax.broadcasted_iota(jnp.int32, (blk, blk), 0))

    def key_rows(ref, kb):
        return ref[0, pl.ds(pl.multiple_of(kb * blk, blk), blk), :]

    def sub_rows(a, part):
        return slice(a * blk + part.start, a * blk + part.stop)

    def visit(blocks, part, carries, accs, on_diagonal):
        n_rows = part.stop - part.start
        carries, accs = list(carries), list(accs)
        zs, sps, suffixes, ws = ([None] * nsub for _ in range(4))

        def scores(a):
            zs[a] = _dot_nt(q_ref[0, sub_rows(a, part), :], key_rows(k_ref, blocks[a]))

        def softplus2(a):
            z = zs[a]
            sp = jnp.maximum(z, 0.0) + jnp.log2(1.0 + jnp.exp2(jnp.minimum(z, -z)))
            if on_diagonal:
                visible = strictly_causal[part, :]
                sp = jnp.where(visible, sp, 0.0)
                zs[a] = jnp.where(visible, z, MASKED)
            sps[a] = sp.astype(BF16)

        def suffix_sum(a):
            suffixes[a] = _dot(sps[a], tri)

        def weights(a):
            d = jnp.minimum(zs[a] - suffixes[a], 0.0)
            carry = jnp.concatenate([carries[a]] * (blk // LANES), axis=1)
            ws[a] = jnp.exp2(d - carry).astype(BF16)
            carries[a] = carries[a] + jnp.broadcast_to(suffixes[a][:, 0:1], (n_rows, LANES))

        def accumulate(a):
            accs[a] = accs[a] + _dot(ws[a], key_rows(v_ref, blocks[a]))

        stages = (scores, softplus2, suffix_sum, weights, accumulate)
        for t in range(nsub + len(stages) - 1):
            for a in range(nsub):
                if 0 <= t - a < len(stages):
                    stages[t - a](a)
        return carries, accs

    def unfinished(carries, next_step):
        lowest = None
        for a in range(nsub):
            c = jnp.min(carries[a].reshape(-1, SUBLANES, LANES), axis=0)
            c = jnp.where(diag[a] - next_step < 0, EXHAUSTED, c)
            lowest = c if lowest is None else jnp.minimum(lowest, c)
        return jnp.min(lowest) < F32_EXP2_UNDERFLOW

    def walk_back(part, carries, accs):
        def body(state):
            _, step, carries, accs = state
            carries = [jnp.where(diag[a] - step < 0, EXHAUSTED, carries[a]) for a in range(nsub)]
            blocks = [jnp.maximum(diag[a] - step, 0) for a in range(nsub)]
            carries, accs = visit(blocks, part, carries, accs, on_diagonal=False)
            return unfinished(carries, step + 1), step + 1, tuple(carries), tuple(accs)

        state = (unfinished(carries, 1), jnp.int32(1), tuple(carries), tuple(accs))
        return lax.while_loop(lambda state: state[0], body, state)[3]

    whole = slice(0, blk)
    carries = [jnp.zeros((blk, LANES), F32) for _ in range(nsub)]
    accs = [jnp.zeros((blk, SB_HD), F32) for _ in range(nsub)]
    carries, accs = visit(diag, whole, carries, accs, on_diagonal=True)
    for part in (slice(0, SB_NEAR_ROWS), slice(SB_NEAR_ROWS, blk)):
        part_accs = walk_back(part, [c[part] for c in carries], [x[part] for x in accs])
        for a in range(nsub):
            rows_a = sub_rows(a, part)
            o_ref[0, rows_a, :] = (part_accs[a] * gate_ref[0, rows_a, :]).astype(BF16)


def _stick_breaking(q, k, v, gate):
    b, s, _ = q.shape
    blk = SB_BLOCK
    qblk = SB_BLOCK * SB_SUBTILES
    tri = (np.arange(blk)[:, None] >= np.arange(blk)[None, :]).astype(np.float32)
    qspec = pl.BlockSpec((1, qblk, SB_HD), lambda bi, h, i: (bi, i, h))
    kvspec = pl.BlockSpec((1, s, SB_HD), lambda bi, h, i: (bi, 0, h))
    return pl.pallas_call(
        _sb_kernel,
        grid=(b, SB_HEADS, s // qblk),
        in_specs=[qspec, kvspec, kvspec, qspec, pl.BlockSpec((blk, blk), lambda bi, h, i: (0, 0))],
        out_specs=qspec,
        out_shape=jax.ShapeDtypeStruct((b, s, SB_W), BF16),
        compiler_params=pltpu.CompilerParams(
            dimension_semantics=("parallel", "parallel", "arbitrary"),
            vmem_limit_bytes=VMEM_LIMIT_BYTES),
        name="stick_breaking",
    )(q, k, v, gate, jnp.asarray(tri, BF16))


def _out_proj_kernel(x_ref, mc_ref, ms_ref, mg_ref, w_ref, o_ref):
    cw = mc_ref.shape[1]
    y = _dot(mc_ref[...], w_ref[0:cw, :].astype(BF16))
    y = y + _dot(ms_ref[...], w_ref[cw:cw + SB_W, :].astype(BF16))
    y = y + _dot(mg_ref[...], w_ref[cw + SB_W:cw + SB_W + GLA_W, :].astype(BF16))
    o_ref[...] = x_ref[...] + y


def _out_proj(x2, mixc, mixs, mixg, w_out, layer):
    m, d = x2.shape
    row = lambda i: (i, 0)
    return pl.pallas_call(
        _out_proj_kernel,
        grid=(m // ROW_TILE,),
        in_specs=[
            pl.BlockSpec((ROW_TILE, d), row),
            pl.BlockSpec((ROW_TILE, mixc.shape[1]), row),
            pl.BlockSpec((ROW_TILE, SB_W), row),
            pl.BlockSpec((ROW_TILE, GLA_W), row),
            pl.BlockSpec((None,) + w_out.shape[1:], lambda i: (layer, 0, 0), pipeline_mode=pl.Buffered(1)),
        ],
        out_specs=pl.BlockSpec((ROW_TILE, d), row),
        out_shape=jax.ShapeDtypeStruct((m, d), F32),
        compiler_params=pltpu.CompilerParams(
            dimension_semantics=("parallel",), vmem_limit_bytes=VMEM_LIMIT_BYTES),
        name="out_proj",
    )(x2, mixc, mixs, mixg, w_out)


def _layer(x2, batch, seq, layer, w_in, w_lr, attn_first, gla_first, w_out, norm_g, conv_w, conv_b,
           sb_qn_g, sb_kn_g, gla_w_up, gla_b_up, gla_on_g):
    pad = LANES - GLA_LR
    wup = jnp.concatenate([gla_w_up, jnp.zeros((pad, GLA_KW), gla_w_up.dtype)], axis=0).astype(BF16)
    row2 = lambda a: a.reshape(1, -1)
    q, k, v, sgate = _attn_proj(x2, row2(norm_g), w_in, layer, attn_first, row2(sb_qn_g), row2(sb_kn_g))
    mixc, glq, glk, glv, ggate, gdec = _conv_gla_proj(
        x2, row2(norm_g), w_in, gla_first, w_lr, layer, conv_w, row2(conv_b), wup, row2(gla_b_up),
        batch=batch, seq=seq)

    r3 = lambda a: a.reshape(batch, seq, a.shape[-1])
    mixg = _gla(r3(glq), r3(glk), r3(glv), r3(gdec), r3(ggate), row2(gla_on_g))
    mixs = _stick_breaking(r3(q), r3(k), r3(v), r3(sgate))
    m = batch * seq
    return _out_proj(x2, mixc, mixs.reshape(m, SB_W), mixg.reshape(m, GLA_W), w_out, layer)


def kernel(x, norm_g, w_in, conv_w, conv_b, sb_qn_g, sb_kn_g, gla_w_up, gla_b_up, gla_on_g, w_out):
    batch, seq, d = x.shape
    assert seq % ROW_TILE == 0 and seq % (SB_BLOCK * SB_SUBTILES) == 0
    assert seq % (GLA_CHUNK * GLA_STEP_CHUNKS) == 0
    attn_first = 4 * conv_w.shape[2]
    gla_first = attn_first + 4 * SB_W
    lr_first = gla_first + 2 * GLA_KW + 2 * GLA_W
    assert w_in.shape[2] - lr_first == GLA_LR
    w_lr = jnp.pad(w_in[:, :, lr_first:], ((0, 0), (0, 0), (0, LANES - GLA_LR)))
    h = x.reshape(batch * seq, d)
    for l in range(norm_g.shape[0]):
        h = _layer(h, batch, seq, l, w_in, w_lr, attn_first, gla_first, w_out, norm_g[l], conv_w[l],
                   conv_b[l], sb_qn_g[l], sb_kn_g[l], gla_w_up[l], gla_b_up[l], gla_on_g[l])
    return h.reshape(batch, seq, d)
```

```python
import functools

import numpy as np
import jax
import jax.numpy as jnp
from jax import lax
from jax.experimental import pallas as pl
from jax.experimental.pallas import tpu as pltpu

SB_HEADS = 8
SB_HD = 128
SB_W = SB_HEADS * SB_HD
GLA_HEADS = 4
GLA_DV = 128
GLA_DK = 64
GLA_W = GLA_HEADS * GLA_DV
GLA_KW = GLA_HEADS * GLA_DK
GLA_LR = 16
GLA_TAU = 16.0
CONV_K = 3
EPS = 1e-6

LANES = 128
SUBLANES = 8
VMEM_LIMIT_BYTES = 56 * 1024 * 1024

ROW_TILE = 512
COL_CHUNK = 512
GLA_CHUNK = 128
GLA_STEP_CHUNKS = 4
GLA_LEVELS = 7
GLA_MATMUL_LEVELS = 3
SB_BLOCK = 256
SB_SUBTILES = 16
SB_NEAR_ROWS = 160
MASKED = -1e30
EXHAUSTED = 1e30
F32_EXP2_UNDERFLOW = 150.0
LOG2_E = 1.4426950408889634

F32 = jnp.float32
BF16 = jnp.bfloat16


def _dot(a, b):
    return jnp.dot(a, b, preferred_element_type=F32)


def _dot_nt(a, b):
    return lax.dot_general(a, b, (((1,), (1,)), ((), ())), preferred_element_type=F32)


def _silu(a):
    return a / (1.0 + jnp.exp(-a))


def _normed_input(x_ref, ng_ref):
    xf = x_ref[...]
    ms = jnp.mean(xf * xf, axis=-1, keepdims=True)
    return (xf * lax.rsqrt(ms + EPS) * ng_ref[...]).astype(BF16)


def _head_rmsnorm(a, gain):
    outs = []
    for j in range(a.shape[1] // LANES):
        aj = a[:, j * LANES:(j + 1) * LANES]
        ms = jnp.mean(aj * aj, axis=-1, keepdims=True)
        outs.append(aj * lax.rsqrt(ms + EPS) * gain)
    return outs


def _attn_proj_kernel(x_ref, ng_ref, wqk_ref, wvg_ref, qg_ref, kg_ref, q_ref, k_ref, v_ref, g_ref):
    h = _normed_input(x_ref, ng_ref)
    q_scale = SB_HD ** -0.5 * LOG2_E
    n_chunks = SB_W // COL_CHUNK
    for c in range(n_chunks):
        cols = slice(c * COL_CHUNK, (c + 1) * COL_CHUNK)
        qa = _dot(h, wqk_ref[:, c * COL_CHUNK:(c + 1) * COL_CHUNK])
        for j, qn in enumerate(_head_rmsnorm(qa, qg_ref[...])):
            q_ref[:, c * COL_CHUNK + j * LANES:c * COL_CHUNK + (j + 1) * LANES] = (qn * q_scale).astype(BF16)
        ka = _dot(h, wqk_ref[:, SB_W + c * COL_CHUNK:SB_W + (c + 1) * COL_CHUNK])
        for j, kn in enumerate(_head_rmsnorm(ka, kg_ref[...])):
            k_ref[:, c * COL_CHUNK + j * LANES:c * COL_CHUNK + (j + 1) * LANES] = kn.astype(BF16)
        va = _dot(h, wvg_ref[:, c * COL_CHUNK:(c + 1) * COL_CHUNK])
        v_ref[:, cols] = va.astype(BF16)
        ga = _dot(h, wvg_ref[:, SB_W + c * COL_CHUNK:SB_W + (c + 1) * COL_CHUNK])
        g_ref[:, cols] = _silu(ga)


def _attn_proj(x2, norm_g, w_in, layer, first_col, qn_g, kn_g):
    m, d = x2.shape
    row = lambda i: (i, 0)
    const = lambda i: (0, 0)
    wblk = 2 * SB_W
    assert first_col % wblk == 0
    wspec = lambda n: pl.BlockSpec((None, d, wblk), lambda i: (layer, 0, first_col // wblk + n),
                                   pipeline_mode=pl.Buffered(1))
    return pl.pallas_call(
        _attn_proj_kernel,
        grid=(m // ROW_TILE,),
        in_specs=[
            pl.BlockSpec((ROW_TILE, d), row),
            pl.BlockSpec((1, d), const),
            wspec(0),
            wspec(1),
            pl.BlockSpec((1, SB_HD), const),
            pl.BlockSpec((1, SB_HD), const),
        ],
        out_specs=[pl.BlockSpec((ROW_TILE, SB_W), row)] * 4,
        out_shape=[
            jax.ShapeDtypeStruct((m, SB_W), BF16),
            jax.ShapeDtypeStruct((m, SB_W), BF16),
            jax.ShapeDtypeStruct((m, SB_W), BF16),
            jax.ShapeDtypeStruct((m, SB_W), F32),
        ],
        compiler_params=pltpu.CompilerParams(
            dimension_semantics=("parallel",), vmem_limit_bytes=VMEM_LIMIT_BYTES),
        name="attn_proj",
    )(x2, norm_g, w_in, w_in, qn_g, kn_g)


def _conv_gla_proj_kernel(x_ref, ng_ref, wc_ref, wg_ref, wlr_ref, cw_ref, cb_ref, wup_ref, bup_ref,
                          mixc_ref, glq_ref, glk_ref, glv_ref, gate_ref, gdec_ref, ubuf_ref,
                          *, conv_w):
    tm = x_ref.shape[0]
    halo = SUBLANES

    @pl.when(pl.program_id(1) == 0)
    def _():
        ubuf_ref[0:halo, :] = jnp.zeros((halo, conv_w), F32)

    h = _normed_input(x_ref, ng_ref)
    proj = lambda w_ref, first, width: _dot(h, w_ref[:, first:first + width])
    cvh = proj(wc_ref, 0, conv_w)
    cvb = proj(wc_ref, conv_w, conv_w)
    cvc = proj(wc_ref, 2 * conv_w, conv_w)
    cvg = proj(wc_ref, 3 * conv_w, conv_w)
    u = cvc * cvh
    ubuf_ref[halo:halo + tm, :] = u
    um1 = ubuf_ref[halo - 1:halo - 1 + tm, :]
    um2 = ubuf_ref[halo - 2:halo - 2 + tm, :]
    y = cw_ref[0:1, :] * um2 + cw_ref[1:2, :] * um1 + cw_ref[2:3, :] * u + cb_ref[...]
    mixc_ref[...] = (cvb * y * _silu(cvg)).astype(BF16)
    ubuf_ref[0:halo, :] = u[tm - halo:tm, :]

    glq_ref[...] = proj(wg_ref, 0, GLA_KW)
    glk_ref[...] = proj(wg_ref, GLA_KW, GLA_KW)
    glv_ref[...] = proj(wg_ref, 2 * GLA_KW, GLA_W).astype(BF16)
    gate_ref[...] = _silu(proj(wg_ref, 2 * GLA_KW + GLA_W, GLA_W))
    lr = proj(wlr_ref, 0, LANES)
    pre = _dot(lr.astype(BF16), wup_ref[...]) + bup_ref[...]
    log_sig = jnp.minimum(pre, 0.0) - jnp.log(1.0 + jnp.exp(-jnp.abs(pre)))
    gdec_ref[...] = log_sig / GLA_TAU


def _conv_gla_proj(x2, norm_g, w_in, gla_first, w_lr, layer, conv_w, conv_b, wup, bup, *, batch, seq):
    m, d = x2.shape
    cw = conv_w.shape[1]
    n_tiles = seq // ROW_TILE
    row = lambda b, j: (b * n_tiles + j, 0)
    const = lambda b, j: (0, 0)
    wconst = lambda b, j: (layer, 0, 0)
    gla_cols = 2 * GLA_KW + 2 * GLA_W
    assert gla_first % gla_cols == 0
    widths = [(cw, BF16), (GLA_KW, F32), (GLA_KW, F32), (GLA_W, BF16), (GLA_W, F32), (GLA_KW, F32)]
    return pl.pallas_call(
        functools.partial(_conv_gla_proj_kernel, conv_w=cw),
        grid=(batch, n_tiles),
        in_specs=[
            pl.BlockSpec((ROW_TILE, d), row),
            pl.BlockSpec((1, d), const),
            pl.BlockSpec((None, d, 4 * cw), wconst, pipeline_mode=pl.Buffered(1)),
            pl.BlockSpec((None, d, gla_cols), lambda b, j: (layer, 0, gla_first // gla_cols),
                         pipeline_mode=pl.Buffered(1)),
            pl.BlockSpec((None, d, LANES), wconst),
            pl.BlockSpec((CONV_K, cw), const),
            pl.BlockSpec((1, cw), const),
            pl.BlockSpec((LANES, GLA_KW), const),
            pl.BlockSpec((1, GLA_KW), const),
        ],
        out_specs=[pl.BlockSpec((ROW_TILE, w), row) for w, _ in widths],
        out_shape=[jax.ShapeDtypeStruct((m, w), dt) for w, dt in widths],
        scratch_shapes=[pltpu.VMEM((ROW_TILE + SUBLANES, cw), F32)],
        compiler_params=pltpu.CompilerParams(
            dimension_semantics=("arbitrary", "arbitrary"), vmem_limit_bytes=VMEM_LIMIT_BYTES),
        name="conv_gla_proj",
    )(x2, norm_g, w_in, w_in, w_lr, conv_w, conv_b, wup, bup)


def _gla_tables():
    c, nl = GLA_CHUNK, GLA_LEVELS
    t = np.arange(c)
    nm = GLA_MATMUL_LEVELS
    sel = np.zeros((nm + 1, c, c), np.float32)
    msk = np.zeros((nl + 1, c, c), np.float32)
    for l in range(nl):
        up = ((t >> l) & 1) == 1
        lo_start = (t >> l) << l
        nxt = ((t >> l) + 1) << l
        for r in range(c if l < nm else 0):
            if up[r]:
                sel[l, r, lo_start[r]:r + 1] = 1.0
            else:
                sel[l, r, r + 1:nxt[r]] = 1.0
        same = (t[:, None] >> (l + 1)) == (t[None, :] >> (l + 1))
        msk[l] = (up[:, None] & ~up[None, :] & same).astype(np.float32)
    sel[nm] = (t[None, :] <= t[:, None]).astype(np.float32)
    msk[nl] = np.eye(c, dtype=np.float32)
    sel = sel.reshape((nm + 1) * c, c)
    msk = np.tile(msk, (1, 1, GLA_HEADS))
    head_of_k = np.arange(GLA_KW) // GLA_DK
    hmask = (head_of_k[None, :] == np.arange(GLA_HEADS)[:, None]).astype(np.float32)
    bd = (head_of_k[:, None] == (np.arange(GLA_W) // GLA_DV)[None, :]).astype(np.float32)
    return sel, msk, hmask.reshape(GLA_HEADS, 1, GLA_KW), bd


def _gla_kernel(q_ref, k_ref, v_ref, g_ref, gate_ref, ong_ref, sel_ref, msk_ref, hm_ref, bd_ref,
                o_ref, state_ref):
    c, nl, nm = GLA_CHUNK, GLA_LEVELS, GLA_MATMUL_LEVELS

    @pl.when(pl.program_id(1) == 0)
    def _():
        state_ref[...] = jnp.zeros_like(state_ref)

    sel = sel_ref[...]

    def stacked_k(kl):
        kb = kl.astype(BF16)
        return jnp.concatenate([kb * hm_ref[hh] for hh in range(GLA_HEADS)], axis=0)

    chunks = range(GLA_STEP_CHUNKS)
    rows = [slice(ci * c, (ci + 1) * c) for ci in chunks]
    e_alls = []
    for ci in chunks:
        g = g_ref[0, rows[ci], :]
        g1 = g.astype(BF16)
        g2 = (g - g1.astype(F32)).astype(BF16)
        e_alls.append(_dot(sel, g1) + _dot(sel, g2))

    def level_exponent(e_all, cum, l):
        if l < nm:
            return e_all[l * c:(l + 1) * c, :]
        slabs = []
        for r0 in range(0, c, SUBLANES):
            slab = cum[r0:r0 + SUBLANES, :]
            if (r0 >> l) & 1:
                first = (r0 >> l) << l
                slabs.append(slab - cum[first - 1:first, :])
            else:
                last_row = (((r0 >> l) + 1) << l) - 1
                slabs.append(cum[last_row:last_row + 1, :] - slab)
        return jnp.concatenate(slabs, axis=0)

    atts, q_decs, updates, decays = [], [], [], []
    for ci in chunks:
        e_all = e_alls[ci]
        cum = e_all[nm * c:(nm + 1) * c, :]
        q = q_ref[0, rows[ci], :] * (GLA_DK ** -0.5)
        k = k_ref[0, rows[ci], :]
        v = v_ref[0, rows[ci], :]
        att = msk_ref[nl] * _dot_nt(q.astype(BF16), stacked_k(k)).astype(BF16)
        for l in range(nl):
            e = jnp.exp(level_exponent(e_all, cum, l))
            att = att + msk_ref[l] * _dot_nt((q * e).astype(BF16), stacked_k(k * e)).astype(BF16)
        atts.append(att)
        q_decs.append((q * jnp.exp(cum)).astype(BF16))
        last = cum[c - 1:c, :]
        k_dec_t = (k * jnp.exp(last - cum)).T
        decay_t = jnp.broadcast_to(jnp.exp(last), (c, GLA_KW)).T
        decays.append(jnp.concatenate([decay_t] * (GLA_W // c), axis=1))
        updates.append(bd_ref[...] * _dot(k_dec_t.astype(BF16), v))

    state = state_ref[...]
    for ci in chunks:
        v = v_ref[0, rows[ci], :]
        o = _dot(q_decs[ci], state.astype(BF16))
        intra = [_dot(atts[ci][:, hh * c:(hh + 1) * c], v[:, hh * GLA_DV:(hh + 1) * GLA_DV])
                 for hh in range(GLA_HEADS)]
        o = o + jnp.concatenate(intra, axis=1)
        state = state * decays[ci] + updates[ci]
        normed = _head_rmsnorm(o, ong_ref[...])
        o_ref[0, rows[ci], :] = (jnp.concatenate(normed, axis=1) * gate_ref[0, rows[ci], :]).astype(BF16)
    state_ref[...] = state


def _gla(glq, glk, glv, gdec, gate, on_g):
    b, s, _ = glq.shape
    sel, msk, hm, bd = _gla_tables()
    c = GLA_CHUNK
    rows = c * GLA_STEP_CHUNKS
    blk = lambda w: pl.BlockSpec((1, rows, w), lambda bi, j: (bi, j, 0))
    const2 = lambda bi, j: (0, 0)
    const3 = lambda bi, j: (0, 0, 0)
    return pl.pallas_call(
        _gla_kernel,
        grid=(b, s // rows),
        in_specs=[
            blk(GLA_KW), blk(GLA_KW), blk(GLA_W), blk(GLA_KW), blk(GLA_W),
            pl.BlockSpec((1, GLA_DV), const2),
            pl.BlockSpec(sel.shape, const2),
            pl.BlockSpec(msk.shape, const3),
            pl.BlockSpec(hm.shape, const3),
            pl.BlockSpec(bd.shape, const2),
        ],
        out_specs=blk(GLA_W),
        out_shape=jax.ShapeDtypeStruct((b, s, GLA_W), BF16),
        scratch_shapes=[pltpu.VMEM((GLA_KW, GLA_W), F32)],
        compiler_params=pltpu.CompilerParams(
            dimension_semantics=("arbitrary", "arbitrary"), vmem_limit_bytes=VMEM_LIMIT_BYTES),
        name="gla",
    )(glq, glk, glv, gdec, gate, on_g, jnp.asarray(sel, BF16), jnp.asarray(msk, BF16), jnp.asarray(hm, BF16),
      jnp.asarray(bd))


def _sb_kernel(q_ref, k_ref, v_ref, gate_ref, tri_ref, o_ref):
    blk, nsub = SB_BLOCK, SB_SUBTILES
    i = pl.program_id(2)
    tri = tri_ref[...]
    diag = [i * nsub + a for a in range(nsub)]
    strictly_causal = (lax.broadcasted_iota(jnp.int32, (blk, blk), 1)
                       < lax.broadcasted_iota(jnp.int32, (blk, blk), 0))

    def key_rows(ref, kb):
        return ref[0, pl.ds(pl.multiple_of(kb * blk, blk), blk), :]

    def sub_rows(a, part):
        return slice(a * blk + part.start, a * blk + part.stop)

    def visit(blocks, part, carries, accs, on_diagonal):
        n_rows = part.stop - part.start
        carries, accs = list(carries), list(accs)
        zs, sps, suffixes, ws = ([None] * nsub for _ in range(4))

        def scores(a):
            zs[a] = _dot_nt(q_ref[0, sub_rows(a, part), :], key_rows(k_ref, blocks[a]))

        def softplus2(a):
            z = zs[a]
            sp = jnp.maximum(z, 0.0) + jnp.log2(1.0 + jnp.exp2(jnp.minimum(z, -z)))
            if on_diagonal:
                visible = strictly_causal[part, :]
                sp = jnp.where(visible, sp, 0.0)
                zs[a] = jnp.where(visible, z, MASKED)
            sps[a] = sp.astype(BF16)

        def suffix_sum(a):
            suffixes[a] = _dot(sps[a], tri)

        def weights(a):
            d = jnp.minimum(zs[a] - suffixes[a], 0.0)
            carry = jnp.concatenate([carries[a]] * (blk // LANES), axis=1)
            ws[a] = jnp.exp2(d - carry).astype(BF16)
            carries[a] = carries[a] + jnp.broadcast_to(suffixes[a][:, 0:1], (n_rows, LANES))

        def accumulate(a):
            accs[a] = accs[a] + _dot(ws[a], key_rows(v_ref, blocks[a]))

        stages = (scores, softplus2, suffix_sum, weights, accumulate)
        for t in range(nsub + len(stages) - 1):
            for a in range(nsub):
                if 0 <= t - a < len(stages):
                    stages[t - a](a)
        return carries, accs

    def unfinished(carries, next_step):
        lowest = None
        for a in range(nsub):
            c = jnp.min(carries[a].reshape(-1, SUBLANES, LANES), axis=0)
            c = jnp.where(diag[a] - next_step < 0, EXHAUSTED, c)
            lowest = c if lowest is None else jnp.minimum(lowest, c)
        return jnp.min(lowest) < F32_EXP2_UNDERFLOW

    def walk_back(part, carries, accs):
        def body(state):
            _, step, carries, accs = state
            carries = [jnp.where(diag[a] - step < 0, EXHAUSTED, carries[a]) for a in range(nsub)]
            blocks = [jnp.maximum(diag[a] - step, 0) for a in range(nsub)]
            carries, accs = visit(blocks, part, carries, accs, on_diagonal=False)
            return unfinished(carries, step + 1), step + 1, tuple(carries), tuple(accs)

        state = (unfinished(carries, 1), jnp.int32(1), tuple(carries), tuple(accs))
        return lax.while_loop(lambda state: state[0], body, state)[3]

    carries, accs = visit(diag, slice(0, blk), [jnp.zeros((blk, LANES), F32)] * nsub,
                          [jnp.zeros((blk, SB_HD), F32)] * nsub, on_diagonal=True)
    for part in (slice(0, SB_NEAR_ROWS), slice(SB_NEAR_ROWS, blk)):
        part_accs = walk_back(part, [c[part] for c in carries], [x[part] for x in accs])
        for a in range(nsub):
            rows_a = sub_rows(a, part)
            o_ref[0, rows_a, :] = (part_accs[a] * gate_ref[0, rows_a, :]).astype(BF16)


def _stick_breaking(q, k, v, gate):
    b, s, _ = q.shape
    blk = SB_BLOCK
    qblk = SB_BLOCK * SB_SUBTILES
    tri = (np.arange(blk)[:, None] >= np.arange(blk)[None, :]).astype(np.float32)
    qspec = pl.BlockSpec((1, qblk, SB_HD), lambda bi, h, i: (bi, i, h))
    kvspec = pl.BlockSpec((1, s, SB_HD), lambda bi, h, i: (bi, 0, h))
    return pl.pallas_call(
        _sb_kernel,
        grid=(b, SB_HEADS, s // qblk),
        in_specs=[qspec, kvspec, kvspec, qspec, pl.BlockSpec((blk, blk), lambda bi, h, i: (0, 0))],
        out_specs=qspec,
        out_shape=jax.ShapeDtypeStruct((b, s, SB_W), BF16),
        compiler_params=pltpu.CompilerParams(
            dimension_semantics=("parallel", "parallel", "arbitrary"),
            vmem_limit_bytes=VMEM_LIMIT_BYTES),
        name="stick_breaking",
    )(q, k, v, gate, jnp.asarray(tri, BF16))


def _out_proj_kernel(x_ref, mc_ref, ms_ref, mg_ref, w_ref, o_ref):
    cw = mc_ref.shape[1]
    y = _dot(mc_ref[...], w_ref[0:cw, :].astype(BF16))
    y = y + _dot(ms_ref[...], w_ref[cw:cw + SB_W, :].astype(BF16))
    y = y + _dot(mg_ref[...], w_ref[cw + SB_W:cw + SB_W + GLA_W, :].astype(BF16))
    o_ref[...] = x_ref[...] + y


def _out_proj(x2, mixc, mixs, mixg, w_out, layer):
    m, d = x2.shape
    row = lambda i: (i, 0)
    return pl.pallas_call(
        _out_proj_kernel,
        grid=(m // ROW_TILE,),
        in_specs=[
            pl.BlockSpec((ROW_TILE, d), row),
            pl.BlockSpec((ROW_TILE, mixc.shape[1]), row),
            pl.BlockSpec((ROW_TILE, SB_W), row),
            pl.BlockSpec((ROW_TILE, GLA_W), row),
            pl.BlockSpec((None,) + w_out.shape[1:], lambda i: (layer, 0, 0), pipeline_mode=pl.Buffered(1)),
        ],
        out_specs=pl.BlockSpec((ROW_TILE, d), row),
        out_shape=jax.ShapeDtypeStruct((m, d), F32),
        compiler_params=pltpu.CompilerParams(
            dimension_semantics=("parallel",), vmem_limit_bytes=VMEM_LIMIT_BYTES),
        name="out_proj",
    )(x2, mixc, mixs, mixg, w_out)


def _layer(x2, batch, seq, layer, w_in, w_lr, attn_first, gla_first, w_out, norm_g, conv_w, conv_b,
           sb_qn_g, sb_kn_g, gla_w_up, gla_b_up, gla_on_g):
    pad = LANES - GLA_LR
    wup = jnp.concatenate([gla_w_up, jnp.zeros((pad, GLA_KW), gla_w_up.dtype)], axis=0).astype(BF16)
    row2 = lambda a: a.reshape(1, -1)
    q, k, v, sgate = _attn_proj(x2, row2(norm_g), w_in, layer, attn_first, row2(sb_qn_g), row2(sb_kn_g))
    mixc, glq, glk, glv, ggate, gdec = _conv_gla_proj(
        x2, row2(norm_g), w_in, gla_first, w_lr, layer, conv_w, row2(conv_b), wup, row2(gla_b_up),
        batch=batch, seq=seq)

    r3 = lambda a: a.reshape(batch, seq, a.shape[-1])
    mixg = _gla(r3(glq), r3(glk), r3(glv), r3(gdec), r3(ggate), row2(gla_on_g))
    mixs = _stick_breaking(r3(q), r3(k), r3(v), r3(sgate))
    m = batch * seq
    return _out_proj(x2, mixc, mixs.reshape(m, SB_W), mixg.reshape(m, GLA_W), w_out, layer)


def kernel(x, norm_g, w_in, conv_w, conv_b, sb_qn_g, sb_kn_g, gla_w_up, gla_b_up, gla_on_g, w_out):
    batch, seq, d = x.shape
    assert seq % ROW_TILE == 0 and seq % (SB_BLOCK * SB_SUBTILES) == 0
    assert seq % (GLA_CHUNK * GLA_STEP_CHUNKS) == 0
    attn_first = 4 * conv_w.shape[2]
    gla_first = attn_first + 4 * SB_W
    lr_first = gla_first + 2 * GLA_KW + 2 * GLA_W
    assert w_in.shape[2] - lr_first == GLA_LR
    w_in_b = w_in.astype(BF16)
    w_lr = jnp.pad(w_in[:, :, lr_first:], ((0, 0), (0, 0), (0, LANES - GLA_LR))).astype(BF16)
    h = x.reshape(batch * seq, d)
    for l in range(norm_g.shape[0]):
        h = _layer(h, batch, seq, l, w_in_b, w_lr, attn_first, gla_first, w_out, norm_g[l], conv_w[l],
                   conv_b[l], sb_qn_g[l], sb_kn_g[l], gla_w_up[l], gla_b_up[l], gla_on_g[l])
    return h.reshape(batch, seq, d)
```

```python
import functools

import numpy as np
import jax
import jax.numpy as jnp
from jax import lax
from jax.experimental import pallas as pl
from jax.experimental.pallas import tpu as pltpu

SB_HEADS = 8
SB_HD = 128
SB_W = SB_HEADS * SB_HD
GLA_HEADS = 4
GLA_DV = 128
GLA_DK = 64
GLA_W = GLA_HEADS * GLA_DV
GLA_KW = GLA_HEADS * GLA_DK
GLA_LR = 16
GLA_TAU = 16.0
CONV_K = 3
EPS = 1e-6

LANES = 128
SUBLANES = 8
VMEM_LIMIT_BYTES = 56 * 1024 * 1024

ROW_TILE = 512
COL_CHUNK = 512
GLA_CHUNK = 128
GLA_STEP_CHUNKS = 4
GLA_LEVELS = 7
GLA_MATMUL_LEVELS = 3
SB_BLOCK = 256
SB_SUBTILES = 16
SB_NEAR_ROWS = 160
MASKED = -1e30
EXHAUSTED = 1e30
F32_EXP2_UNDERFLOW = 150.0
LOG2_E = 1.4426950408889634

F32 = jnp.float32
BF16 = jnp.bfloat16


def _dot(a, b):
    return jnp.dot(a, b, preferred_element_type=F32)


def _dot_nt(a, b):
    return lax.dot_general(a, b, (((1,), (1,)), ((), ())), preferred_element_type=F32)


def _silu(a):
    return a / (1.0 + jnp.exp(-a))


def _normed_input(x_ref, ng_ref):
    xf = x_ref[...]
    ms = jnp.mean(xf * xf, axis=-1, keepdims=True)
    return (xf * lax.rsqrt(ms + EPS) * ng_ref[...]).astype(BF16)


def _head_rmsnorm(a, gain):
    outs = []
    for j in range(a.shape[1] // LANES):
        aj = a[:, j * LANES:(j + 1) * LANES]
        ms = jnp.mean(aj * aj, axis=-1, keepdims=True)
        outs.append(aj * lax.rsqrt(ms + EPS) * gain)
    return outs


def _attn_proj_kernel(x_ref, ng_ref, wqk_ref, wvg_ref, qg_ref, kg_ref, q_ref, k_ref, v_ref, g_ref):
    h = _normed_input(x_ref, ng_ref)
    q_scale = SB_HD ** -0.5 * LOG2_E
    n_chunks = SB_W // COL_CHUNK
    for c in range(n_chunks):
        cols = slice(c * COL_CHUNK, (c + 1) * COL_CHUNK)
        qa = _dot(h, wqk_ref[:, c * COL_CHUNK:(c + 1) * COL_CHUNK])
        for j, qn in enumerate(_head_rmsnorm(qa, qg_ref[...])):
            q_ref[:, c * COL_CHUNK + j * LANES:c * COL_CHUNK + (j + 1) * LANES] = (qn * q_scale).astype(BF16)
        ka = _dot(h, wqk_ref[:, SB_W + c * COL_CHUNK:SB_W + (c + 1) * COL_CHUNK])
        for j, kn in enumerate(_head_rmsnorm(ka, kg_ref[...])):
            k_ref[:, c * COL_CHUNK + j * LANES:c * COL_CHUNK + (j + 1) * LANES] = kn.astype(BF16)
        va = _dot(h, wvg_ref[:, c * COL_CHUNK:(c + 1) * COL_CHUNK])
        v_ref[:, cols] = va.astype(BF16)
        ga = _dot(h, wvg_ref[:, SB_W + c * COL_CHUNK:SB_W + (c + 1) * COL_CHUNK])
        g_ref[:, cols] = _silu(ga)


def _attn_proj(x2, norm_g, w_in, layer, first_col, qn_g, kn_g):
    m, d = x2.shape
    row = lambda i: (i, 0)
    const = lambda i: (0, 0)
    wblk = 2 * SB_W
    assert first_col % wblk == 0
    wspec = lambda n: pl.BlockSpec((None, d, wblk), lambda i: (layer, 0, first_col // wblk + n),
                                   pipeline_mode=pl.Buffered(1))
    return pl.pallas_call(
        _attn_proj_kernel,
        grid=(m // ROW_TILE,),
        in_specs=[
            pl.BlockSpec((ROW_TILE, d), row),
            pl.BlockSpec((1, d), const),
            wspec(0),
            wspec(1),
            pl.BlockSpec((1, SB_HD), const),
            pl.BlockSpec((1, SB_HD), const),
        ],
        out_specs=[pl.BlockSpec((ROW_TILE, SB_W), row)] * 4,
        out_shape=[
            jax.ShapeDtypeStruct((m, SB_W), BF16),
            jax.ShapeDtypeStruct((m, SB_W), BF16),
            jax.ShapeDtypeStruct((m, SB_W), BF16),
            jax.ShapeDtypeStruct((m, SB_W), F32),
        ],
        compiler_params=pltpu.CompilerParams(
            dimension_semantics=("parallel",), vmem_limit_bytes=VMEM_LIMIT_BYTES),
        name="attn_proj",
    )(x2, norm_g, w_in, w_in, qn_g, kn_g)


def _conv_gla_kernel(x_ref, ng_ref, wc_ref, wg_ref, wlr_ref, cw_ref, cb_ref, wup_ref, bup_ref,
                     ong_ref, sel_ref, msk_ref, hm_ref, bd_ref, mixc_ref, mixg_ref, ubuf_ref,
                     state_ref, *, conv_w):
    tm = x_ref.shape[0]
    halo = SUBLANES

    @pl.when(pl.program_id(1) == 0)
    def _():
        ubuf_ref[0:halo, :] = jnp.zeros((halo, conv_w), F32)
        state_ref[...] = jnp.zeros_like(state_ref)

    h = _normed_input(x_ref, ng_ref)
    proj = lambda w_ref, first, width: _dot(h, w_ref[:, first:first + width])
    lr = proj(wlr_ref, 0, LANES)
    pre = _dot(lr.astype(BF16), wup_ref[...]) + bup_ref[...]
    log_decay = (jnp.minimum(pre, 0.0) - jnp.log(1.0 + jnp.exp(-jnp.abs(pre)))) / GLA_TAU
    gl_q = proj(wg_ref, 0, GLA_KW)
    gl_k = proj(wg_ref, GLA_KW, GLA_KW)
    gl_v = proj(wg_ref, 2 * GLA_KW, GLA_W).astype(BF16)
    gate = _silu(proj(wg_ref, 2 * GLA_KW + GLA_W, GLA_W))

    cvh = proj(wc_ref, 0, conv_w)
    cvb = proj(wc_ref, conv_w, conv_w)
    cvc = proj(wc_ref, 2 * conv_w, conv_w)
    cvg = proj(wc_ref, 3 * conv_w, conv_w)
    u = cvc * cvh
    ubuf_ref[halo:halo + tm, :] = u
    um1 = ubuf_ref[halo - 1:halo - 1 + tm, :]
    um2 = ubuf_ref[halo - 2:halo - 2 + tm, :]
    y = cw_ref[0:1, :] * um2 + cw_ref[1:2, :] * um1 + cw_ref[2:3, :] * u + cb_ref[...]
    mixc_ref[...] = (cvb * y * _silu(cvg)).astype(BF16)
    ubuf_ref[0:halo, :] = u[tm - halo:tm, :]

    _gla_chunks(gl_q, gl_k, gl_v, log_decay, gate, ong_ref, sel_ref, msk_ref, hm_ref, bd_ref,
                state_ref, mixg_ref)


def _conv_gla(x2, norm_g, w_in, gla_first, w_lr, layer, conv_w, conv_b, wup, bup, on_g, *, batch, seq):
    m, d = x2.shape
    cw = conv_w.shape[1]
    assert ROW_TILE == GLA_CHUNK * GLA_STEP_CHUNKS
    n_tiles = seq // ROW_TILE
    row = lambda b, j: (b * n_tiles + j, 0)
    const = lambda b, j: (0, 0)
    const3 = lambda b, j: (0, 0, 0)
    wconst = lambda b, j: (layer, 0, 0)
    gla_cols = 2 * GLA_KW + 2 * GLA_W
    assert gla_first % gla_cols == 0
    sel, msk, hm, bd = _gla_tables()
    return pl.pallas_call(
        functools.partial(_conv_gla_kernel, conv_w=cw),
        grid=(batch, n_tiles),
        in_specs=[
            pl.BlockSpec((ROW_TILE, d), row),
            pl.BlockSpec((1, d), const),
            pl.BlockSpec((None, d, 4 * cw), wconst, pipeline_mode=pl.Buffered(1)),
            pl.BlockSpec((None, d, gla_cols), lambda b, j: (layer, 0, gla_first // gla_cols),
                         pipeline_mode=pl.Buffered(1)),
            pl.BlockSpec((None, d, LANES), wconst),
            pl.BlockSpec((CONV_K, cw), const),
            pl.BlockSpec((1, cw), const),
            pl.BlockSpec((LANES, GLA_KW), const),
            pl.BlockSpec((1, GLA_KW), const),
            pl.BlockSpec((1, GLA_DV), const),
            pl.BlockSpec(sel.shape, const),
            pl.BlockSpec(msk.shape, const3),
            pl.BlockSpec(hm.shape, const3),
            pl.BlockSpec(bd.shape, const),
        ],
        out_specs=[pl.BlockSpec((ROW_TILE, cw), row), pl.BlockSpec((ROW_TILE, GLA_W), row)],
        out_shape=[jax.ShapeDtypeStruct((m, cw), BF16), jax.ShapeDtypeStruct((m, GLA_W), BF16)],
        scratch_shapes=[pltpu.VMEM((ROW_TILE + SUBLANES, cw), F32), pltpu.VMEM((GLA_KW, GLA_W), F32)],
        compiler_params=pltpu.CompilerParams(
            dimension_semantics=("arbitrary", "arbitrary"), vmem_limit_bytes=VMEM_LIMIT_BYTES),
        name="conv_gla",
    )(x2, norm_g, w_in, w_in, w_lr, conv_w, conv_b, wup, bup, on_g, jnp.asarray(sel, BF16),
      jnp.asarray(msk, BF16), jnp.asarray(hm, BF16), jnp.asarray(bd))


def _gla_tables():
    c, nl = GLA_CHUNK, GLA_LEVELS
    t = np.arange(c)
    nm = GLA_MATMUL_LEVELS
    sel = np.zeros((nm + 1, c, c), np.float32)
    msk = np.zeros((nl + 1, c, c), np.float32)
    for l in range(nl):
        up = ((t >> l) & 1) == 1
        lo_start = (t >> l) << l
        nxt = ((t >> l) + 1) << l
        for r in range(c if l < nm else 0):
            if up[r]:
                sel[l, r, lo_start[r]:r + 1] = 1.0
            else:
                sel[l, r, r + 1:nxt[r]] = 1.0
        same = (t[:, None] >> (l + 1)) == (t[None, :] >> (l + 1))
        msk[l] = (up[:, None] & ~up[None, :] & same).astype(np.float32)
    sel[nm] = (t[None, :] <= t[:, None]).astype(np.float32)
    msk[nl] = np.eye(c, dtype=np.float32)
    sel = sel.reshape((nm + 1) * c, c)
    msk = np.tile(msk, (1, 1, GLA_HEADS))
    head_of_k = np.arange(GLA_KW) // GLA_DK
    hmask = (head_of_k[None, :] == np.arange(GLA_HEADS)[:, None]).astype(np.float32)
    bd = (head_of_k[:, None] == (np.arange(GLA_W) // GLA_DV)[None, :]).astype(np.float32)
    return sel, msk, hmask.reshape(GLA_HEADS, 1, GLA_KW), bd


def _gla_chunks(q_all, k_all, v_all, g_all, gate_all, ong_ref, sel_ref, msk_ref, hm_ref, bd_ref,
                state_ref, o_ref):
    c, nl, nm = GLA_CHUNK, GLA_LEVELS, GLA_MATMUL_LEVELS
    sel = sel_ref[...]

    def stacked_k(kl):
        kb = kl.astype(BF16)
        return jnp.concatenate([kb * hm_ref[hh] for hh in range(GLA_HEADS)], axis=0)

    chunks = range(GLA_STEP_CHUNKS)
    rows = [slice(ci * c, (ci + 1) * c) for ci in chunks]
    e_alls = []
    for ci in chunks:
        g = g_all[rows[ci], :]
        g1 = g.astype(BF16)
        g2 = (g - g1.astype(F32)).astype(BF16)
        e_alls.append(_dot(sel, g1) + _dot(sel, g2))

    def level_exponent(e_all, cum, l):
        if l < nm:
            return e_all[l * c:(l + 1) * c, :]
        slabs = []
        for r0 in range(0, c, SUBLANES):
            slab = cum[r0:r0 + SUBLANES, :]
            if (r0 >> l) & 1:
                first = (r0 >> l) << l
                slabs.append(slab - cum[first - 1:first, :])
            else:
                last_row = (((r0 >> l) + 1) << l) - 1
                slabs.append(cum[last_row:last_row + 1, :] - slab)
        return jnp.concatenate(slabs, axis=0)

    atts, q_decs, updates, decays = [], [], [], []
    for ci in chunks:
        e_all = e_alls[ci]
        cum = e_all[nm * c:(nm + 1) * c, :]
        q = q_all[rows[ci], :] * (GLA_DK ** -0.5)
        k = k_all[rows[ci], :]
        v = v_all[rows[ci], :]
        att = msk_ref[nl] * _dot_nt(q.astype(BF16), stacked_k(k)).astype(BF16)
        for l in range(nl):
            e = jnp.exp(level_exponent(e_all, cum, l))
            att = att + msk_ref[l] * _dot_nt((q * e).astype(BF16), stacked_k(k * e)).astype(BF16)
        atts.append(att)
        q_decs.append((q * jnp.exp(cum)).astype(BF16))
        last = cum[c - 1:c, :]
        k_dec_t = (k * jnp.exp(last - cum)).T
        decay_t = jnp.broadcast_to(jnp.exp(last), (c, GLA_KW)).T
        decays.append(jnp.concatenate([decay_t] * (GLA_W // c), axis=1))
        updates.append(bd_ref[...] * _dot(k_dec_t.astype(BF16), v))

    state = state_ref[...]
    for ci in chunks:
        v = v_all[rows[ci], :]
        o = _dot(q_decs[ci], state.astype(BF16))
        intra = [_dot(atts[ci][:, hh * c:(hh + 1) * c], v[:, hh * GLA_DV:(hh + 1) * GLA_DV])
                 for hh in range(GLA_HEADS)]
        o = o + jnp.concatenate(intra, axis=1)
        state = state * decays[ci] + updates[ci]
        normed = _head_rmsnorm(o, ong_ref[...])
        o_ref[rows[ci], :] = (jnp.concatenate(normed, axis=1) * gate_all[rows[ci], :]).astype(BF16)
    state_ref[...] = state


def _sb_kernel(q_ref, k_ref, v_ref, gate_ref, tri_ref, o_ref):
    blk, nsub = SB_BLOCK, SB_SUBTILES
    i = pl.program_id(2)
    tri = tri_ref[...]
    diag = [i * nsub + a for a in range(nsub)]
    strictly_causal = (lax.broadcasted_iota(jnp.int32, (blk, blk), 1)
                       < lax.broadcasted_iota(jnp.int32, (blk, blk), 0))

    def key_rows(ref, kb):
        return ref[0, pl.ds(pl.multiple_of(kb * blk, blk), blk), :]

    def sub_rows(a, part):
        return slice(a * blk + part.start, a * blk + part.stop)

    def visit(blocks, part, carries, accs, on_diagonal):
        n_rows = part.stop - part.start
        carries, accs = list(carries), list(accs)
        zs, sps, suffixes, ws = ([None] * nsub for _ in range(4))

        def scores(a):
            zs[a] = _dot_nt(q_ref[0, sub_rows(a, part), :], key_rows(k_ref, blocks[a]))

        def softplus2(a):
            z = zs[a]
            sp = jnp.maximum(z, 0.0) + jnp.log2(1.0 + jnp.exp2(jnp.minimum(z, -z)))
            if on_diagonal:
                visible = strictly_causal[part, :]
                sp = jnp.where(visible, sp, 0.0)
                zs[a] = jnp.where(visible, z, MASKED)
            sps[a] = sp.astype(BF16)

        def suffix_sum(a):
            suffixes[a] = _dot(sps[a], tri)

        def weights(a):
            d = jnp.minimum(zs[a] - suffixes[a], 0.0)
            carry = jnp.concatenate([carries[a]] * (blk // LANES), axis=1)
            ws[a] = jnp.exp2(d - carry).astype(BF16)
            carries[a] = carries[a] + jnp.broadcast_to(suffixes[a][:, 0:1], (n_rows, LANES))

        def accumulate(a):
            accs[a] = accs[a] + _dot(ws[a], key_rows(v_ref, blocks[a]))

        stages = (scores, softplus2, suffix_sum, weights, accumulate)
        for t in range(nsub + len(stages) - 1):
            for a in range(nsub):
                if 0 <= t - a < len(stages):
                    stages[t - a](a)
        return carries, accs

    def unfinished(carries, next_step):
        lowest = None
        for a in range(nsub):
            c = jnp.min(carries[a].reshape(-1, SUBLANES, LANES), axis=0)
            c = jnp.where(diag[a] - next_step < 0, EXHAUSTED, c)
            lowest = c if lowest is None else jnp.minimum(lowest, c)
        return jnp.min(lowest) < F32_EXP2_UNDERFLOW

    def walk_back(part, carries, accs):
        def body(state):
            _, step, carries, accs = state
            carries = [jnp.where(diag[a] - step < 0, EXHAUSTED, carries[a]) for a in range(nsub)]
            blocks = [jnp.maximum(diag[a] - step, 0) for a in range(nsub)]
            carries, accs = visit(blocks, part, carries, accs, on_diagonal=False)
            return unfinished(carries, step + 1), step + 1, tuple(carries), tuple(accs)

        state = (unfinished(carries, 1), jnp.int32(1), tuple(carries), tuple(accs))
        return lax.while_loop(lambda state: state[0], body, state)[3]

    carries, accs = visit(diag, slice(0, blk), [jnp.zeros((blk, LANES), F32)] * nsub,
                          [jnp.zeros((blk, SB_HD), F32)] * nsub, on_diagonal=True)
    for part in (slice(0, SB_NEAR_ROWS), slice(SB_NEAR_ROWS, blk)):
        part_accs = walk_back(part, [c[part] for c in carries], [x[part] for x in accs])
        for a in range(nsub):
            rows_a = sub_rows(a, part)
            o_ref[0, rows_a, :] = (part_accs[a] * gate_ref[0, rows_a, :]).astype(BF16)


def _stick_breaking(q, k, v, gate):
    b, s, _ = q.shape
    blk = SB_BLOCK
    qblk = SB_BLOCK * SB_SUBTILES
    tri = (np.arange(blk)[:, None] >= np.arange(blk)[None, :]).astype(np.float32)
    qspec = pl.BlockSpec((1, qblk, SB_HD), lambda bi, h, i: (bi, i, h))
    kvspec = pl.BlockSpec((1, s, SB_HD), lambda bi, h, i: (bi, 0, h))
    return pl.pallas_call(
        _sb_kernel,
        grid=(b, SB_HEADS, s // qblk),
        in_specs=[qspec, kvspec, kvspec, qspec, pl.BlockSpec((blk, blk), lambda bi, h, i: (0, 0))],
        out_specs=qspec,
        out_shape=jax.ShapeDtypeStruct((b, s, SB_W), BF16),
        compiler_params=pltpu.CompilerParams(
            dimension_semantics=("parallel", "parallel", "arbitrary"),
            vmem_limit_bytes=VMEM_LIMIT_BYTES),
        name="stick_breaking",
    )(q, k, v, gate, jnp.asarray(tri, BF16))


def _out_proj_kernel(x_ref, mc_ref, ms_ref, mg_ref, w_ref, o_ref):
    cw = mc_ref.shape[1]
    y = _dot(mc_ref[...], w_ref[0:cw, :].astype(BF16))
    y = y + _dot(ms_ref[...], w_ref[cw:cw + SB_W, :].astype(BF16))
    y = y + _dot(mg_ref[...], w_ref[cw + SB_W:cw + SB_W + GLA_W, :].astype(BF16))
    o_ref[...] = x_ref[...] + y


def _out_proj(x2, mixc, mixs, mixg, w_out, layer):
    m, d = x2.shape
    row = lambda i: (i, 0)
    return pl.pallas_call(
        _out_proj_kernel,
        grid=(m // ROW_TILE,),
        in_specs=[
            pl.BlockSpec((ROW_TILE, d), row),
            pl.BlockSpec((ROW_TILE, mixc.shape[1]), row),
            pl.BlockSpec((ROW_TILE, SB_W), row),
            pl.BlockSpec((ROW_TILE, GLA_W), row),
            pl.BlockSpec((None,) + w_out.shape[1:], lambda i: (layer, 0, 0), pipeline_mode=pl.Buffered(1)),
        ],
        out_specs=pl.BlockSpec((ROW_TILE, d), row),
        out_shape=jax.ShapeDtypeStruct((m, d), F32),
        compiler_params=pltpu.CompilerParams(
            dimension_semantics=("parallel",), vmem_limit_bytes=VMEM_LIMIT_BYTES),
        name="out_proj",
    )(x2, mixc, mixs, mixg, w_out)


def _layer(x2, batch, seq, layer, w_in, w_lr, attn_first, gla_first, w_out, norm_g, conv_w, conv_b,
           sb_qn_g, sb_kn_g, gla_w_up, gla_b_up, gla_on_g):
    pad = LANES - GLA_LR
    wup = jnp.concatenate([gla_w_up, jnp.zeros((pad, GLA_KW), gla_w_up.dtype)], axis=0).astype(BF16)
    row2 = lambda a: a.reshape(1, -1)
    q, k, v, sgate = _attn_proj(x2, row2(norm_g), w_in, layer, attn_first, row2(sb_qn_g), row2(sb_kn_g))
    mixc, mixg = _conv_gla(
        x2, row2(norm_g), w_in, gla_first, w_lr, layer, conv_w, row2(conv_b), wup, row2(gla_b_up),
        row2(gla_on_g), batch=batch, seq=seq)

    r3 = lambda a: a.reshape(batch, seq, a.shape[-1])
    mixs = _stick_breaking(r3(q), r3(k), r3(v), r3(sgate))
    return _out_proj(x2, mixc, mixs.reshape(batch * seq, SB_W), mixg, w_out, layer)


def kernel(x, norm_g, w_in, conv_w, conv_b, sb_qn_g, sb_kn_g, gla_w_up, gla_b_up, gla_on_g, w_out):
    batch, seq, d = x.shape
    assert seq % ROW_TILE == 0 and seq % (SB_BLOCK * SB_SUBTILES) == 0
    assert seq % (GLA_CHUNK * GLA_STEP_CHUNKS) == 0
    attn_first = 4 * conv_w.shape[2]
    gla_first = attn_first + 4 * SB_W
    lr_first = gla_first + 2 * GLA_KW + 2 * GLA_W
    assert w_in.shape[2] - lr_first == GLA_LR
    w_in_b = w_in.astype(BF16)
    w_lr = jnp.pad(w_in[:, :, lr_first:], ((0, 0), (0, 0), (0, LANES - GLA_LR))).astype(BF16)
    h = x.reshape(batch * seq, d)
    for l in range(norm_g.shape[0]):
        h = _layer(h, batch, seq, l, w_in_b, w_lr, attn_first, gla_first, w_out, norm_g[l], conv_w[l],
                   conv_b[l], sb_qn_g[l], sb_kn_g[l], gla_w_up[l], gla_b_up[l], gla_on_g[l])
    return h.reshape(batch, seq, d)
```

```python
import functools

import numpy as np
import jax
import jax.numpy as jnp
from jax import lax
from jax.experimental import pallas as pl
from jax.experimental.pallas import tpu as pltpu

SB_HEADS = 8
SB_HD = 128
SB_W = SB_HEADS * SB_HD
GLA_HEADS = 4
GLA_DV = 128
GLA_DK = 64
GLA_W = GLA_HEADS * GLA_DV
GLA_KW = GLA_HEADS * GLA_DK
GLA_LR = 16
GLA_TAU = 16.0
CONV_K = 3
EPS = 1e-6

LANES = 128
SUBLANES = 8
VMEM_LIMIT_BYTES = 56 * 1024 * 1024

ROW_TILE = 512
COL_CHUNK = 512
GLA_CHUNK = 128
GLA_STEP_CHUNKS = 4
GLA_LEVELS = 7
GLA_MATMUL_LEVELS = 3
SB_BLOCK = 256
SB_SUBTILES = 32
SB_NEAR_ROWS = 160
MASKED = -1e30
EXHAUSTED = 1e30
F32_EXP2_UNDERFLOW = 150.0
LOG2_E = 1.4426950408889634

F32 = jnp.float32
BF16 = jnp.bfloat16


def _dot(a, b):
    return jnp.dot(a, b, preferred_element_type=F32)


def _dot_nt(a, b):
    return lax.dot_general(a, b, (((1,), (1,)), ((), ())), preferred_element_type=F32)


def _silu(a):
    return a / (1.0 + jnp.exp(-a))


def _normed_input(x_ref, ng_ref):
    xf = x_ref[...]
    ms = jnp.mean(xf * xf, axis=-1, keepdims=True)
    return (xf * lax.rsqrt(ms + EPS) * ng_ref[...]).astype(BF16)


def _head_rmsnorm(a, gain):
    outs = []
    for j in range(a.shape[1] // LANES):
        aj = a[:, j * LANES:(j + 1) * LANES]
        ms = jnp.mean(aj * aj, axis=-1, keepdims=True)
        outs.append(aj * lax.rsqrt(ms + EPS) * gain)
    return outs


def _attn_proj_kernel(x_ref, ng_ref, wqk_ref, wvg_ref, qg_ref, kg_ref, q_ref, k_ref, v_ref, g_ref):
    h = _normed_input(x_ref, ng_ref)
    q_scale = SB_HD ** -0.5 * LOG2_E
    n_chunks = SB_W // COL_CHUNK
    for c in range(n_chunks):
        cols = slice(c * COL_CHUNK, (c + 1) * COL_CHUNK)
        qa = _dot(h, wqk_ref[:, c * COL_CHUNK:(c + 1) * COL_CHUNK])
        for j, qn in enumerate(_head_rmsnorm(qa, qg_ref[...])):
            q_ref[:, c * COL_CHUNK + j * LANES:c * COL_CHUNK + (j + 1) * LANES] = (qn * q_scale).astype(BF16)
        ka = _dot(h, wqk_ref[:, SB_W + c * COL_CHUNK:SB_W + (c + 1) * COL_CHUNK])
        for j, kn in enumerate(_head_rmsnorm(ka, kg_ref[...])):
            k_ref[:, c * COL_CHUNK + j * LANES:c * COL_CHUNK + (j + 1) * LANES] = kn.astype(BF16)
        va = _dot(h, wvg_ref[:, c * COL_CHUNK:(c + 1) * COL_CHUNK])
        v_ref[:, cols] = va.astype(BF16)
        ga = _dot(h, wvg_ref[:, SB_W + c * COL_CHUNK:SB_W + (c + 1) * COL_CHUNK])
        g_ref[:, cols] = _silu(ga)


def _attn_proj(x2, norm_g, w_in, layer, first_col, qn_g, kn_g):
    m, d = x2.shape
    row = lambda i: (i, 0)
    const = lambda i: (0, 0)
    wblk = 2 * SB_W
    assert first_col % wblk == 0
    wspec = lambda n: pl.BlockSpec((None, d, wblk), lambda i: (layer, 0, first_col // wblk + n),
                                   pipeline_mode=pl.Buffered(1))
    return pl.pallas_call(
        _attn_proj_kernel,
        grid=(m // ROW_TILE,),
        in_specs=[
            pl.BlockSpec((ROW_TILE, d), row),
            pl.BlockSpec((1, d), const),
            wspec(0),
            wspec(1),
            pl.BlockSpec((1, SB_HD), const),
            pl.BlockSpec((1, SB_HD), const),
        ],
        out_specs=[pl.BlockSpec((ROW_TILE, SB_W), row)] * 4,
        out_shape=[
            jax.ShapeDtypeStruct((m, SB_W), BF16),
            jax.ShapeDtypeStruct((m, SB_W), BF16),
            jax.ShapeDtypeStruct((m, SB_W), BF16),
            jax.ShapeDtypeStruct((m, SB_W), F32),
        ],
        compiler_params=pltpu.CompilerParams(
            dimension_semantics=("parallel",), vmem_limit_bytes=VMEM_LIMIT_BYTES),
        name="attn_proj",
    )(x2, norm_g, w_in, w_in, qn_g, kn_g)


def _conv_gla_kernel(x_ref, ng_ref, wc_ref, wg_ref, wlr_ref, cw_ref, cb_ref, wup_ref, bup_ref,
                     ong_ref, sel_ref, msk_ref, hm_ref, bd_ref, mixc_ref, mixg_ref, ubuf_ref,
                     state_ref, *, conv_w):
    tm = x_ref.shape[0]
    halo = SUBLANES

    @pl.when(pl.program_id(1) == 0)
    def _():
        ubuf_ref[0:halo, :] = jnp.zeros((halo, conv_w), F32)
        state_ref[...] = jnp.zeros_like(state_ref)

    h = _normed_input(x_ref, ng_ref)
    proj = lambda w_ref, first, width: _dot(h, w_ref[:, first:first + width])
    lr = proj(wlr_ref, 0, LANES)
    pre = _dot(lr.astype(BF16), wup_ref[...]) + bup_ref[...]
    log_decay = (jnp.minimum(pre, 0.0) - jnp.log(1.0 + jnp.exp(-jnp.abs(pre)))) / GLA_TAU
    gl_q = proj(wg_ref, 0, GLA_KW)
    gl_k = proj(wg_ref, GLA_KW, GLA_KW)
    gl_v = proj(wg_ref, 2 * GLA_KW, GLA_W).astype(BF16)
    gate = _silu(proj(wg_ref, 2 * GLA_KW + GLA_W, GLA_W))

    cvh = proj(wc_ref, 0, conv_w)
    cvb = proj(wc_ref, conv_w, conv_w)
    cvc = proj(wc_ref, 2 * conv_w, conv_w)
    cvg = proj(wc_ref, 3 * conv_w, conv_w)
    u = cvc * cvh
    ubuf_ref[halo:halo + tm, :] = u
    um1 = ubuf_ref[halo - 1:halo - 1 + tm, :]
    um2 = ubuf_ref[halo - 2:halo - 2 + tm, :]
    y = cw_ref[0:1, :] * um2 + cw_ref[1:2, :] * um1 + cw_ref[2:3, :] * u + cb_ref[...]
    mixc_ref[...] = (cvb * y * _silu(cvg)).astype(BF16)
    ubuf_ref[0:halo, :] = u[tm - halo:tm, :]

    _gla_chunks(gl_q, gl_k, gl_v, log_decay, gate, ong_ref, sel_ref, msk_ref, hm_ref, bd_ref,
                state_ref, mixg_ref)


def _conv_gla(x2, norm_g, w_in, gla_first, w_lr, layer, conv_w, conv_b, wup, bup, on_g, *, batch, seq):
    m, d = x2.shape
    cw = conv_w.shape[1]
    assert ROW_TILE == GLA_CHUNK * GLA_STEP_CHUNKS
    n_tiles = seq // ROW_TILE
    row = lambda b, j: (b * n_tiles + j, 0)
    const = lambda b, j: (0, 0)
    const3 = lambda b, j: (0, 0, 0)
    wconst = lambda b, j: (layer, 0, 0)
    gla_cols = 2 * GLA_KW + 2 * GLA_W
    assert gla_first % gla_cols == 0
    sel, msk, hm, bd = _gla_tables()
    return pl.pallas_call(
        functools.partial(_conv_gla_kernel, conv_w=cw),
        grid=(batch, n_tiles),
        in_specs=[
            pl.BlockSpec((ROW_TILE, d), row),
            pl.BlockSpec((1, d), const),
            pl.BlockSpec((None, d, 4 * cw), wconst, pipeline_mode=pl.Buffered(1)),
            pl.BlockSpec((None, d, gla_cols), lambda b, j: (layer, 0, gla_first // gla_cols),
                         pipeline_mode=pl.Buffered(1)),
            pl.BlockSpec((None, d, LANES), wconst),
            pl.BlockSpec((CONV_K, cw), const),
            pl.BlockSpec((1, cw), const),
            pl.BlockSpec((LANES, GLA_KW), const),
            pl.BlockSpec((1, GLA_KW), const),
            pl.BlockSpec((1, GLA_DV), const),
            pl.BlockSpec(sel.shape, const),
            pl.BlockSpec(msk.shape, const3),
            pl.BlockSpec(hm.shape, const3),
            pl.BlockSpec(bd.shape, const),
        ],
        out_specs=[pl.BlockSpec((ROW_TILE, cw), row), pl.BlockSpec((ROW_TILE, GLA_W), row)],
        out_shape=[jax.ShapeDtypeStruct((m, cw), BF16), jax.ShapeDtypeStruct((m, GLA_W), BF16)],
        scratch_shapes=[pltpu.VMEM((ROW_TILE + SUBLANES, cw), F32), pltpu.VMEM((GLA_KW, GLA_W), F32)],
        compiler_params=pltpu.CompilerParams(
            dimension_semantics=("arbitrary", "arbitrary"), vmem_limit_bytes=VMEM_LIMIT_BYTES),
        name="conv_gla",
    )(x2, norm_g, w_in, w_in, w_lr, conv_w, conv_b, wup, bup, on_g, jnp.asarray(sel, BF16),
      jnp.asarray(msk, BF16), jnp.asarray(hm, BF16), jnp.asarray(bd))


def _gla_tables():
    c, nl = GLA_CHUNK, GLA_LEVELS
    t = np.arange(c)
    nm = GLA_MATMUL_LEVELS
    sel = np.zeros((nm + 1, c, c), np.float32)
    msk = np.zeros((nl + 1, c, c), np.float32)
    for l in range(nl):
        up = ((t >> l) & 1) == 1
        lo_start = (t >> l) << l
        nxt = ((t >> l) + 1) << l
        for r in range(c if l < nm else 0):
            if up[r]:
                sel[l, r, lo_start[r]:r + 1] = 1.0
            else:
                sel[l, r, r + 1:nxt[r]] = 1.0
        same = (t[:, None] >> (l + 1)) == (t[None, :] >> (l + 1))
        msk[l] = (up[:, None] & ~up[None, :] & same).astype(np.float32)
    sel[nm] = (t[None, :] <= t[:, None]).astype(np.float32)
    msk[nl] = np.eye(c, dtype=np.float32)
    sel = sel.reshape((nm + 1) * c, c)
    msk = np.tile(msk, (1, 1, GLA_HEADS))
    head_of_k = np.arange(GLA_KW) // GLA_DK
    hmask = (head_of_k[None, :] == np.arange(GLA_HEADS)[:, None]).astype(np.float32)
    bd = (head_of_k[:, None] == (np.arange(GLA_W) // GLA_DV)[None, :]).astype(np.float32)
    return sel, msk, hmask.reshape(GLA_HEADS, 1, GLA_KW), bd


def _gla_chunks(q_all, k_all, v_all, g_all, gate_all, ong_ref, sel_ref, msk_ref, hm_ref, bd_ref,
                state_ref, o_ref):
    c, nl, nm = GLA_CHUNK, GLA_LEVELS, GLA_MATMUL_LEVELS
    sel = sel_ref[...]

    def stacked_k(kl):
        kb = kl.astype(BF16)
        return jnp.concatenate([kb * hm_ref[hh] for hh in range(GLA_HEADS)], axis=0)

    chunks = range(GLA_STEP_CHUNKS)
    rows = [slice(ci * c, (ci + 1) * c) for ci in chunks]
    e_alls = []
    for ci in chunks:
        g = g_all[rows[ci], :]
        g1 = g.astype(BF16)
        g2 = (g - g1.astype(F32)).astype(BF16)
        e_alls.append(_dot(sel, g1) + _dot(sel, g2))

    def level_exponent(e_all, cum, l):
        if l < nm:
            return e_all[l * c:(l + 1) * c, :]
        slabs = []
        for r0 in range(0, c, SUBLANES):
            slab = cum[r0:r0 + SUBLANES, :]
            if (r0 >> l) & 1:
                first = (r0 >> l) << l
                slabs.append(slab - cum[first - 1:first, :])
            else:
                last_row = (((r0 >> l) + 1) << l) - 1
                slabs.append(cum[last_row:last_row + 1, :] - slab)
        return jnp.concatenate(slabs, axis=0)

    atts, q_decs, updates, decays = [], [], [], []
    for ci in chunks:
        e_all = e_alls[ci]
        cum = e_all[nm * c:(nm + 1) * c, :]
        q = q_all[rows[ci], :] * (GLA_DK ** -0.5)
        k = k_all[rows[ci], :]
        v = v_all[rows[ci], :]
        att = msk_ref[nl] * _dot_nt(q.astype(BF16), stacked_k(k)).astype(BF16)
        for l in range(nl):
            e = jnp.exp(level_exponent(e_all, cum, l))
            att = att + msk_ref[l] * _dot_nt((q * e).astype(BF16), stacked_k(k * e)).astype(BF16)
        atts.append(att)
        q_decs.append((q * jnp.exp(cum)).astype(BF16))
        last = cum[c - 1:c, :]
        k_dec_t = (k * jnp.exp(last - cum)).T
        decay_t = jnp.broadcast_to(jnp.exp(last), (c, GLA_KW)).T
        decays.append(jnp.concatenate([decay_t] * (GLA_W // c), axis=1))
        updates.append(bd_ref[...] * _dot(k_dec_t.astype(BF16), v))

    state = state_ref[...]
    for ci in chunks:
        v = v_all[rows[ci], :]
        o = _dot(q_decs[ci], state.astype(BF16))
        intra = [_dot(atts[ci][:, hh * c:(hh + 1) * c], v[:, hh * GLA_DV:(hh + 1) * GLA_DV])
                 for hh in range(GLA_HEADS)]
        o = o + jnp.concatenate(intra, axis=1)
        state = state * decays[ci] + updates[ci]
        normed = _head_rmsnorm(o, ong_ref[...])
        o_ref[rows[ci], :] = (jnp.concatenate(normed, axis=1) * gate_all[rows[ci], :]).astype(BF16)
    state_ref[...] = state


def _sb_kernel(q_ref, k_ref, v_ref, gate_ref, tri_ref, o_ref):
    blk, nsub = SB_BLOCK, SB_SUBTILES
    i = pl.program_id(2)
    tri = tri_ref[...]
    diag = [i * nsub + a for a in range(nsub)]
    strictly_causal = (lax.broadcasted_iota(jnp.int32, (blk, blk), 1)
                       < lax.broadcasted_iota(jnp.int32, (blk, blk), 0))

    def key_rows(ref, kb):
        return ref[0, pl.ds(pl.multiple_of(kb * blk, blk), blk), :]

    def sub_rows(a, part):
        return slice(a * blk + part.start, a * blk + part.stop)

    def visit(blocks, part, carries, accs, on_diagonal):
        n_rows = part.stop - part.start
        carries, accs = list(carries), list(accs)
        zs, sps, suffixes, ws = ([None] * nsub for _ in range(4))

        def scores(a):
            zs[a] = _dot_nt(q_ref[0, sub_rows(a, part), :], key_rows(k_ref, blocks[a]))

        def softplus2(a):
            z = zs[a]
            sp = jnp.maximum(z, 0.0) + jnp.log2(1.0 + jnp.exp2(jnp.minimum(z, -z)))
            if on_diagonal:
                visible = strictly_causal[part, :]
                sp = jnp.where(visible, sp, 0.0)
                zs[a] = jnp.where(visible, z, MASKED)
            sps[a] = sp.astype(BF16)

        def suffix_sum(a):
            suffixes[a] = _dot(sps[a], tri)

        def weights(a):
            d = jnp.minimum(zs[a] - suffixes[a], 0.0)
            carry = jnp.concatenate([carries[a]] * (blk // LANES), axis=1)
            ws[a] = jnp.exp2(d - carry).astype(BF16)
            carries[a] = carries[a] + jnp.broadcast_to(suffixes[a][:, 0:1], (n_rows, LANES))

        def accumulate(a):
            accs[a] = accs[a] + _dot(ws[a], key_rows(v_ref, blocks[a]))

        stages = (scores, softplus2, suffix_sum, weights, accumulate)
        for t in range(nsub + len(stages) - 1):
            for a in range(nsub):
                if 0 <= t - a < len(stages):
                    stages[t - a](a)
        return carries, accs

    def unfinished(carries, next_step):
        lowest = None
        for a in range(nsub):
            c = jnp.min(carries[a].reshape(-1, SUBLANES, LANES), axis=0)
            c = jnp.where(diag[a] - next_step < 0, EXHAUSTED, c)
            lowest = c if lowest is None else jnp.minimum(lowest, c)
        return jnp.min(lowest) < F32_EXP2_UNDERFLOW

    def walk_back(part, carries, accs):
        def body(state):
            _, step, carries, accs = state
            carries = [jnp.where(diag[a] - step < 0, EXHAUSTED, carries[a]) for a in range(nsub)]
            blocks = [jnp.maximum(diag[a] - step, 0) for a in range(nsub)]
            carries, accs = visit(blocks, part, carries, accs, on_diagonal=False)
            return unfinished(carries, step + 1), step + 1, tuple(carries), tuple(accs)

        state = (unfinished(carries, 1), jnp.int32(1), tuple(carries), tuple(accs))
        return lax.while_loop(lambda state: state[0], body, state)[3]

    carries, accs = visit(diag, slice(0, blk), [jnp.zeros((blk, LANES), F32)] * nsub,
                          [jnp.zeros((blk, SB_HD), F32)] * nsub, on_diagonal=True)
    for part in (slice(0, SB_NEAR_ROWS), slice(SB_NEAR_ROWS, blk)):
        part_accs = walk_back(part, [c[part] for c in carries], [x[part] for x in accs])
        for a in range(nsub):
            rows_a = sub_rows(a, part)
            o_ref[0, rows_a, :] = (part_accs[a] * gate_ref[0, rows_a, :]).astype(BF16)


def _stick_breaking(q, k, v, gate):
    b, s, _ = q.shape
    blk = SB_BLOCK
    qblk = SB_BLOCK * SB_SUBTILES
    tri = (np.arange(blk)[:, None] >= np.arange(blk)[None, :]).astype(np.float32)
    qspec = pl.BlockSpec((1, qblk, SB_HD), lambda bi, h, i: (bi, i, h))
    kvspec = pl.BlockSpec((1, s, SB_HD), lambda bi, h, i: (bi, 0, h))
    return pl.pallas_call(
        _sb_kernel,
        grid=(b, SB_HEADS, s // qblk),
        in_specs=[qspec, kvspec, kvspec, qspec, pl.BlockSpec((blk, blk), lambda bi, h, i: (0, 0))],
        out_specs=qspec,
        out_shape=jax.ShapeDtypeStruct((b, s, SB_W), BF16),
        compiler_params=pltpu.CompilerParams(
            dimension_semantics=("parallel", "parallel", "arbitrary"),
            vmem_limit_bytes=VMEM_LIMIT_BYTES),
        name="stick_breaking",
    )(q, k, v, gate, jnp.asarray(tri, BF16))


def _out_proj_kernel(x_ref, mc_ref, ms_ref, mg_ref, w_ref, o_ref):
    cw = mc_ref.shape[1]
    y = _dot(mc_ref[...], w_ref[0:cw, :].astype(BF16))
    y = y + _dot(ms_ref[...], w_ref[cw:cw + SB_W, :].astype(BF16))
    y = y + _dot(mg_ref[...], w_ref[cw + SB_W:cw + SB_W + GLA_W, :].astype(BF16))
    o_ref[...] = x_ref[...] + y


def _out_proj(x2, mixc, mixs, mixg, w_out, layer):
    m, d = x2.shape
    row = lambda i: (i, 0)
    return pl.pallas_call(
        _out_proj_kernel,
        grid=(m // ROW_TILE,),
        in_specs=[
            pl.BlockSpec((ROW_TILE, d), row),
            pl.BlockSpec((ROW_TILE, mixc.shape[1]), row),
            pl.BlockSpec((ROW_TILE, SB_W), row),
            pl.BlockSpec((ROW_TILE, GLA_W), row),
            pl.BlockSpec((None,) + w_out.shape[1:], lambda i: (layer, 0, 0), pipeline_mode=pl.Buffered(1)),
        ],
        out_specs=pl.BlockSpec((ROW_TILE, d), row),
        out_shape=jax.ShapeDtypeStruct((m, d), F32),
        compiler_params=pltpu.CompilerParams(
            dimension_semantics=("parallel",), vmem_limit_bytes=VMEM_LIMIT_BYTES),
        name="out_proj",
    )(x2, mixc, mixs, mixg, w_out)


def _layer(x2, batch, seq, layer, w_in, w_lr, attn_first, gla_first, w_out, norm_g, conv_w, conv_b,
           sb_qn_g, sb_kn_g, gla_w_up, gla_b_up, gla_on_g):
    pad = LANES - GLA_LR
    wup = jnp.concatenate([gla_w_up, jnp.zeros((pad, GLA_KW), gla_w_up.dtype)], axis=0).astype(BF16)
    row2 = lambda a: a.reshape(1, -1)
    q, k, v, sgate = _attn_proj(x2, row2(norm_g), w_in, layer, attn_first, row2(sb_qn_g), row2(sb_kn_g))
    mixc, mixg = _conv_gla(
        x2, row2(norm_g), w_in, gla_first, w_lr, layer, conv_w, row2(conv_b), wup, row2(gla_b_up),
        row2(gla_on_g), batch=batch, seq=seq)

    r3 = lambda a: a.reshape(batch, seq, a.shape[-1])
    mixs = _stick_breaking(r3(q), r3(k), r3(v), r3(sgate))
    return _out_proj(x2, mixc, mixs.reshape(batch * seq, SB_W), mixg, w_out, layer)


def kernel(x, norm_g, w_in, conv_w, conv_b, sb_qn_g, sb_kn_g, gla_w_up, gla_b_up, gla_on_g, w_out):
    batch, seq, d = x.shape
    assert seq % ROW_TILE == 0 and seq % (SB_BLOCK * SB_SUBTILES) == 0
    assert seq % (GLA_CHUNK * GLA_STEP_CHUNKS) == 0
    attn_first = 4 * conv_w.shape[2]
    gla_first = attn_first + 4 * SB_W
    lr_first = gla_first + 2 * GLA_KW + 2 * GLA_W
    assert w_in.shape[2] - lr_first == GLA_LR
    w_in_b = w_in.astype(BF16)
    w_lr = jnp.pad(w_in[:, :, lr_first:], ((0, 0), (0, 0), (0, LANES - GLA_LR))).astype(BF16)
    h = x.reshape(batch * seq, d)
    for l in range(norm_g.shape[0]):
        h = _layer(h, batch, seq, l, w_in_b, w_lr, attn_first, gla_first, w_out, norm_g[l], conv_w[l],
                   conv_b[l], sb_qn_g[l], sb_kn_g[l], gla_w_up[l], gla_b_up[l], gla_on_g[l])
    return h.reshape(batch, seq, d)
```

```python
import functools

import numpy as np
import jax
import jax.numpy as jnp
from jax import lax
from jax.experimental import pallas as pl
from jax.experimental.pallas import tpu as pltpu

SB_HEADS = 8
SB_HD = 128
SB_W = SB_HEADS * SB_HD
GLA_HEADS = 4
GLA_DV = 128
GLA_DK = 64
GLA_W = GLA_HEADS * GLA_DV
GLA_KW = GLA_HEADS * GLA_DK
GLA_LR = 16
GLA_TAU = 16.0
CONV_K = 3
EPS = 1e-6

LANES = 128
SUBLANES = 8
VMEM_LIMIT_BYTES = 56 * 1024 * 1024

ROW_TILE = 512
COL_CHUNK = 256
GLA_CHUNK = 128
GLA_STEP_CHUNKS = 4
GLA_LEVELS = 7
GLA_MATMUL_LEVELS = 3
SB_BLOCK = 256
SB_SUBTILES = 32
SB_NEAR_ROWS = 160
MASKED = -1e30
EXHAUSTED = 1e30
F32_EXP2_UNDERFLOW = 150.0
LOG2_E = 1.4426950408889634

F32 = jnp.float32
BF16 = jnp.bfloat16


def _dot(a, b):
    return jnp.dot(a, b, preferred_element_type=F32)


def _dot_nt(a, b):
    return lax.dot_general(a, b, (((1,), (1,)), ((), ())), preferred_element_type=F32)


def _silu(a):
    return a / (1.0 + jnp.exp(-a))


def _normed_input(x_ref, ng_ref):
    xf = x_ref[...]
    ms = jnp.mean(xf * xf, axis=-1, keepdims=True)
    return (xf * lax.rsqrt(ms + EPS) * ng_ref[...]).astype(BF16)


def _head_rmsnorm(a, gain):
    outs = []
    for j in range(a.shape[1] // LANES):
        aj = a[:, j * LANES:(j + 1) * LANES]
        ms = jnp.mean(aj * aj, axis=-1, keepdims=True)
        outs.append(aj * lax.rsqrt(ms + EPS) * gain)
    return outs


def _attn_proj_kernel(x_ref, ng_ref, wqk_ref, wvg_ref, qg_ref, kg_ref, q_ref, k_ref, v_ref, g_ref):
    h = _normed_input(x_ref, ng_ref)
    q_scale = SB_HD ** -0.5 * LOG2_E
    n_chunks = SB_W // COL_CHUNK
    for c in range(n_chunks):
        cols = slice(c * COL_CHUNK, (c + 1) * COL_CHUNK)
        qa = _dot(h, wqk_ref[:, c * COL_CHUNK:(c + 1) * COL_CHUNK])
        for j, qn in enumerate(_head_rmsnorm(qa, qg_ref[...])):
            q_ref[:, c * COL_CHUNK + j * LANES:c * COL_CHUNK + (j + 1) * LANES] = (qn * q_scale).astype(BF16)
        ka = _dot(h, wqk_ref[:, SB_W + c * COL_CHUNK:SB_W + (c + 1) * COL_CHUNK])
        for j, kn in enumerate(_head_rmsnorm(ka, kg_ref[...])):
            k_ref[:, c * COL_CHUNK + j * LANES:c * COL_CHUNK + (j + 1) * LANES] = kn.astype(BF16)
        va = _dot(h, wvg_ref[:, c * COL_CHUNK:(c + 1) * COL_CHUNK])
        v_ref[:, cols] = va.astype(BF16)
        ga = _dot(h, wvg_ref[:, SB_W + c * COL_CHUNK:SB_W + (c + 1) * COL_CHUNK])
        g_ref[:, cols] = _silu(ga)


def _attn_proj(x2, norm_g, w_in, layer, first_col, qn_g, kn_g):
    m, d = x2.shape
    row = lambda i: (i, 0)
    const = lambda i: (0, 0)
    wblk = 2 * SB_W
    assert first_col % wblk == 0
    wspec = lambda n: pl.BlockSpec((None, d, wblk), lambda i: (layer, 0, first_col // wblk + n),
                                   pipeline_mode=pl.Buffered(1))
    return pl.pallas_call(
        _attn_proj_kernel,
        grid=(m // ROW_TILE,),
        in_specs=[
            pl.BlockSpec((ROW_TILE, d), row),
            pl.BlockSpec((1, d), const),
            wspec(0),
            wspec(1),
            pl.BlockSpec((1, SB_HD), const),
            pl.BlockSpec((1, SB_HD), const),
        ],
        out_specs=[pl.BlockSpec((ROW_TILE, SB_W), row)] * 4,
        out_shape=[
            jax.ShapeDtypeStruct((m, SB_W), BF16),
            jax.ShapeDtypeStruct((m, SB_W), BF16),
            jax.ShapeDtypeStruct((m, SB_W), BF16),
            jax.ShapeDtypeStruct((m, SB_W), F32),
        ],
        compiler_params=pltpu.CompilerParams(
            dimension_semantics=("parallel",), vmem_limit_bytes=VMEM_LIMIT_BYTES),
        name="attn_proj",
    )(x2, norm_g, w_in, w_in, qn_g, kn_g)


def _conv_gla_kernel(x_ref, ng_ref, wc_ref, wg_ref, wlr_ref, cw_ref, cb_ref, wup_ref, bup_ref,
                     ong_ref, sel_ref, msk_ref, hm_ref, bd_ref, mixc_ref, mixg_ref, ubuf_ref,
                     state_ref, *, conv_w):
    tm = x_ref.shape[0]
    halo = SUBLANES

    @pl.when(pl.program_id(1) == 0)
    def _():
        ubuf_ref[0:halo, :] = jnp.zeros((halo, conv_w), F32)
        state_ref[...] = jnp.zeros_like(state_ref)

    h = _normed_input(x_ref, ng_ref)
    proj = lambda w_ref, first, width: _dot(h, w_ref[:, first:first + width])
    lr = proj(wlr_ref, 0, LANES)
    pre = _dot(lr.astype(BF16), wup_ref[...]) + bup_ref[...]
    log_decay = (jnp.minimum(pre, 0.0) - jnp.log(1.0 + jnp.exp(-jnp.abs(pre)))) / GLA_TAU
    gl_q = proj(wg_ref, 0, GLA_KW)
    gl_k = proj(wg_ref, GLA_KW, GLA_KW)
    gl_v = proj(wg_ref, 2 * GLA_KW, GLA_W).astype(BF16)
    gate = _silu(proj(wg_ref, 2 * GLA_KW + GLA_W, GLA_W))

    cvh = proj(wc_ref, 0, conv_w)
    cvb = proj(wc_ref, conv_w, conv_w)
    cvc = proj(wc_ref, 2 * conv_w, conv_w)
    cvg = proj(wc_ref, 3 * conv_w, conv_w)
    u = cvc * cvh
    ubuf_ref[halo:halo + tm, :] = u
    um1 = ubuf_ref[halo - 1:halo - 1 + tm, :]
    um2 = ubuf_ref[halo - 2:halo - 2 + tm, :]
    y = cw_ref[0:1, :] * um2 + cw_ref[1:2, :] * um1 + cw_ref[2:3, :] * u + cb_ref[...]
    mixc_ref[...] = (cvb * y * _silu(cvg)).astype(BF16)
    ubuf_ref[0:halo, :] = u[tm - halo:tm, :]

    _gla_chunks(gl_q, gl_k, gl_v, log_decay, gate, ong_ref, sel_ref, msk_ref, hm_ref, bd_ref,
                state_ref, mixg_ref)


def _conv_gla(x2, norm_g, w_in, gla_first, w_lr, layer, conv_w, conv_b, wup, bup, on_g, *, batch, seq):
    m, d = x2.shape
    cw = conv_w.shape[1]
    assert ROW_TILE == GLA_CHUNK * GLA_STEP_CHUNKS
    n_tiles = seq // ROW_TILE
    row = lambda b, j: (b * n_tiles + j, 0)
    const = lambda b, j: (0, 0)
    const3 = lambda b, j: (0, 0, 0)
    wconst = lambda b, j: (layer, 0, 0)
    gla_cols = 2 * GLA_KW + 2 * GLA_W
    assert gla_first % gla_cols == 0
    sel, msk, hm, bd = _gla_tables()
    return pl.pallas_call(
        functools.partial(_conv_gla_kernel, conv_w=cw),
        grid=(batch, n_tiles),
        in_specs=[
            pl.BlockSpec((ROW_TILE, d), row),
            pl.BlockSpec((1, d), const),
            pl.BlockSpec((None, d, 4 * cw), wconst, pipeline_mode=pl.Buffered(1)),
            pl.BlockSpec((None, d, gla_cols), lambda b, j: (layer, 0, gla_first // gla_cols),
                         pipeline_mode=pl.Buffered(1)),
            pl.BlockSpec((None, d, LANES), wconst),
            pl.BlockSpec((CONV_K, cw), const),
            pl.BlockSpec((1, cw), const),
            pl.BlockSpec((LANES, GLA_KW), const),
            pl.BlockSpec((1, GLA_KW), const),
            pl.BlockSpec((1, GLA_DV), const),
            pl.BlockSpec(sel.shape, const),
            pl.BlockSpec(msk.shape, const3),
            pl.BlockSpec(hm.shape, const3),
            pl.BlockSpec(bd.shape, const),
        ],
        out_specs=[pl.BlockSpec((ROW_TILE, cw), row), pl.BlockSpec((ROW_TILE, GLA_W), row)],
        out_shape=[jax.ShapeDtypeStruct((m, cw), BF16), jax.ShapeDtypeStruct((m, GLA_W), BF16)],
        scratch_shapes=[pltpu.VMEM((ROW_TILE + SUBLANES, cw), F32), pltpu.VMEM((GLA_KW, GLA_W), F32)],
        compiler_params=pltpu.CompilerParams(
            dimension_semantics=("arbitrary", "arbitrary"), vmem_limit_bytes=VMEM_LIMIT_BYTES),
        name="conv_gla",
    )(x2, norm_g, w_in, w_in, w_lr, conv_w, conv_b, wup, bup, on_g, jnp.asarray(sel, BF16),
      jnp.asarray(msk, BF16), jnp.asarray(hm, BF16), jnp.asarray(bd))


def _gla_tables():
    c, nl = GLA_CHUNK, GLA_LEVELS
    t = np.arange(c)
    nm = GLA_MATMUL_LEVELS
    sel = np.zeros((nm + 1, c, c), np.float32)
    msk = np.zeros((nl + 1, c, c), np.float32)
    for l in range(nl):
        up = ((t >> l) & 1) == 1
        lo_start = (t >> l) << l
        nxt = ((t >> l) + 1) << l
        for r in range(c if l < nm else 0):
            if up[r]:
                sel[l, r, lo_start[r]:r + 1] = 1.0
            else:
                sel[l, r, r + 1:nxt[r]] = 1.0
        same = (t[:, None] >> (l + 1)) == (t[None, :] >> (l + 1))
        msk[l] = (up[:, None] & ~up[None, :] & same).astype(np.float32)
    sel[nm] = (t[None, :] <= t[:, None]).astype(np.float32)
    msk[nl] = np.eye(c, dtype=np.float32)
    sel = sel.reshape((nm + 1) * c, c)
    msk = np.tile(msk, (1, 1, GLA_HEADS))
    head_of_k = np.arange(GLA_KW) // GLA_DK
    hmask = (head_of_k[None, :] == np.arange(GLA_HEADS)[:, None]).astype(np.float32)
    bd = (head_of_k[:, None] == (np.arange(GLA_W) // GLA_DV)[None, :]).astype(np.float32)
    return sel, msk, hmask.reshape(GLA_HEADS, 1, GLA_KW), bd


def _gla_chunks(q_all, k_all, v_all, g_all, gate_all, ong_ref, sel_ref, msk_ref, hm_ref, bd_ref,
                state_ref, o_ref):
    c, nl, nm = GLA_CHUNK, GLA_LEVELS, GLA_MATMUL_LEVELS
    sel = sel_ref[...]

    def stacked_k(kl):
        kb = kl.astype(BF16)
        return jnp.concatenate([kb * hm_ref[hh] for hh in range(GLA_HEADS)], axis=0)

    chunks = range(GLA_STEP_CHUNKS)
    rows = [slice(ci * c, (ci + 1) * c) for ci in chunks]
    e_alls = []
    for ci in chunks:
        g = g_all[rows[ci], :]
        g1 = g.astype(BF16)
        g2 = (g - g1.astype(F32)).astype(BF16)
        e_alls.append(_dot(sel, g1) + _dot(sel, g2))

    def level_exponent(e_all, cum, l):
        if l < nm:
            return e_all[l * c:(l + 1) * c, :]
        slabs = []
        for r0 in range(0, c, SUBLANES):
            slab = cum[r0:r0 + SUBLANES, :]
            if (r0 >> l) & 1:
                first = (r0 >> l) << l
                slabs.append(slab - cum[first - 1:first, :])
            else:
                last_row = (((r0 >> l) + 1) << l) - 1
                slabs.append(cum[last_row:last_row + 1, :] - slab)
        return jnp.concatenate(slabs, axis=0)

    atts, q_decs, updates, decays = [], [], [], []
    for ci in chunks:
        e_all = e_alls[ci]
        cum = e_all[nm * c:(nm + 1) * c, :]
        q = q_all[rows[ci], :] * (GLA_DK ** -0.5)
        k = k_all[rows[ci], :]
        v = v_all[rows[ci], :]
        att = msk_ref[nl] * _dot_nt(q.astype(BF16), stacked_k(k)).astype(BF16)
        for l in range(nl):
            e = jnp.exp(level_exponent(e_all, cum, l))
            att = att + msk_ref[l] * _dot_nt((q * e).astype(BF16), stacked_k(k * e)).astype(BF16)
        atts.append(att)
        q_decs.append((q * jnp.exp(cum)).astype(BF16))
        last = cum[c - 1:c, :]
        k_dec_t = (k * jnp.exp(last - cum)).T
        decay_t = jnp.broadcast_to(jnp.exp(last), (c, GLA_KW)).T
        decays.append(jnp.concatenate([decay_t] * (GLA_W // c), axis=1))
        updates.append(bd_ref[...] * _dot(k_dec_t.astype(BF16), v))

    state = state_ref[...]
    for ci in chunks:
        v = v_all[rows[ci], :]
        o = _dot(q_decs[ci], state.astype(BF16))
        intra = [_dot(atts[ci][:, hh * c:(hh + 1) * c], v[:, hh * GLA_DV:(hh + 1) * GLA_DV])
                 for hh in range(GLA_HEADS)]
        o = o + jnp.concatenate(intra, axis=1)
        state = state * decays[ci] + updates[ci]
        normed = _head_rmsnorm(o, ong_ref[...])
        o_ref[rows[ci], :] = (jnp.concatenate(normed, axis=1) * gate_all[rows[ci], :]).astype(BF16)
    state_ref[...] = state


def _sb_kernel(q_ref, k_ref, v_ref, gate_ref, tri_ref, o_ref):
    blk, nsub = SB_BLOCK, SB_SUBTILES
    i = pl.program_id(2)
    tri = tri_ref[...]
    diag = [i * nsub + a for a in range(nsub)]
    strictly_causal = (lax.broadcasted_iota(jnp.int32, (blk, blk), 1)
                       < lax.broadcasted_iota(jnp.int32, (blk, blk), 0))

    def key_rows(ref, kb):
        return ref[0, pl.ds(pl.multiple_of(kb * blk, blk), blk), :]

    def sub_rows(a, part):
        return slice(a * blk + part.start, a * blk + part.stop)

    def visit(blocks, part, carries, accs, on_diagonal):
        n_rows = part.stop - part.start
        carries, accs = list(carries), list(accs)
        zs, sps, suffixes, ws = ([None] * nsub for _ in range(4))

        def scores(a):
            zs[a] = _dot_nt(q_ref[0, sub_rows(a, part), :], key_rows(k_ref, blocks[a]))

        def softplus2(a):
            z = zs[a]
            sp = jnp.maximum(z, 0.0) + jnp.log2(1.0 + jnp.exp2(jnp.minimum(z, -z)))
            if on_diagonal:
                visible = strictly_causal[part, :]
                sp = jnp.where(visible, sp, 0.0)
                zs[a] = jnp.where(visible, z, MASKED)
            sps[a] = sp.astype(BF16)

        def suffix_sum(a):
            suffixes[a] = _dot(sps[a], tri)

        def weights(a):
            d = jnp.minimum(zs[a] - suffixes[a], 0.0)
            carry = jnp.concatenate([carries[a]] * (blk // LANES), axis=1)
            ws[a] = jnp.exp2(d - carry).astype(BF16)
            carries[a] = carries[a] + jnp.broadcast_to(suffixes[a][:, 0:1], (n_rows, LANES))

        def accumulate(a):
            accs[a] = accs[a] + _dot(ws[a], key_rows(v_ref, blocks[a]))

        stages = (scores, softplus2, suffix_sum, weights, accumulate)
        for t in range(nsub + len(stages) - 1):
            for a in range(nsub):
                if 0 <= t - a < len(stages):
                    stages[t - a](a)
        return carries, accs

    def unfinished(carries, next_step):
        lowest = None
        for a in range(nsub):
            c = jnp.min(carries[a].reshape(-1, SUBLANES, LANES), axis=0)
            c = jnp.where(diag[a] - next_step < 0, EXHAUSTED, c)
            lowest = c if lowest is None else jnp.minimum(lowest, c)
        return jnp.min(lowest) < F32_EXP2_UNDERFLOW

    def walk_back(part, carries, accs):
        def body(state):
            _, step, carries, accs = state
            carries = [jnp.where(diag[a] - step < 0, EXHAUSTED, carries[a]) for a in range(nsub)]
            blocks = [jnp.maximum(diag[a] - step, 0) for a in range(nsub)]
            carries, accs = visit(blocks, part, carries, accs, on_diagonal=False)
            return unfinished(carries, step + 1), step + 1, tuple(carries), tuple(accs)

        state = (unfinished(carries, 1), jnp.int32(1), tuple(carries), tuple(accs))
        return lax.while_loop(lambda state: state[0], body, state)[3]

    carries, accs = visit(diag, slice(0, blk), [jnp.zeros((blk, LANES), F32)] * nsub,
                          [jnp.zeros((blk, SB_HD), F32)] * nsub, on_diagonal=True)
    for part in (slice(0, SB_NEAR_ROWS), slice(SB_NEAR_ROWS, blk)):
        part_accs = walk_back(part, [c[part] for c in carries], [x[part] for x in accs])
        for a in range(nsub):
            rows_a = sub_rows(a, part)
            o_ref[0, rows_a, :] = (part_accs[a] * gate_ref[0, rows_a, :]).astype(BF16)


def _stick_breaking(q, k, v, gate):
    b, s, _ = q.shape
    blk = SB_BLOCK
    qblk = SB_BLOCK * SB_SUBTILES
    tri = (np.arange(blk)[:, None] >= np.arange(blk)[None, :]).astype(np.float32)
    qspec = pl.BlockSpec((1, qblk, SB_HD), lambda bi, h, i: (bi, i, h))
    kvspec = pl.BlockSpec((1, s, SB_HD), lambda bi, h, i: (bi, 0, h))
    return pl.pallas_call(
        _sb_kernel,
        grid=(b, SB_HEADS, s // qblk),
        in_specs=[qspec, kvspec, kvspec, qspec, pl.BlockSpec((blk, blk), lambda bi, h, i: (0, 0))],
        out_specs=qspec,
        out_shape=jax.ShapeDtypeStruct((b, s, SB_W), BF16),
        compiler_params=pltpu.CompilerParams(
            dimension_semantics=("parallel", "parallel", "arbitrary"),
            vmem_limit_bytes=VMEM_LIMIT_BYTES),
        name="stick_breaking",
    )(q, k, v, gate, jnp.asarray(tri, BF16))


def _out_proj_kernel(x_ref, mc_ref, ms_ref, mg_ref, w_ref, o_ref):
    cw = mc_ref.shape[1]
    y = _dot(mc_ref[...], w_ref[0:cw, :].astype(BF16))
    y = y + _dot(ms_ref[...], w_ref[cw:cw + SB_W, :].astype(BF16))
    y = y + _dot(mg_ref[...], w_ref[cw + SB_W:cw + SB_W + GLA_W, :].astype(BF16))
    o_ref[...] = x_ref[...] + y


def _out_proj(x2, mixc, mixs, mixg, w_out, layer):
    m, d = x2.shape
    row = lambda i: (i, 0)
    return pl.pallas_call(
        _out_proj_kernel,
        grid=(m // ROW_TILE,),
        in_specs=[
            pl.BlockSpec((ROW_TILE, d), row),
            pl.BlockSpec((ROW_TILE, mixc.shape[1]), row),
            pl.BlockSpec((ROW_TILE, SB_W), row),
            pl.BlockSpec((ROW_TILE, GLA_W), row),
            pl.BlockSpec((None,) + w_out.shape[1:], lambda i: (layer, 0, 0), pipeline_mode=pl.Buffered(1)),
        ],
        out_specs=pl.BlockSpec((ROW_TILE, d), row),
        out_shape=jax.ShapeDtypeStruct((m, d), F32),
        compiler_params=pltpu.CompilerParams(
            dimension_semantics=("parallel",), vmem_limit_bytes=VMEM_LIMIT_BYTES),
        name="out_proj",
    )(x2, mixc, mixs, mixg, w_out)


def _layer(x2, batch, seq, layer, w_in, w_lr, attn_first, gla_first, w_out, norm_g, conv_w, conv_b,
           sb_qn_g, sb_kn_g, gla_w_up, gla_b_up, gla_on_g):
    pad = LANES - GLA_LR
    wup = jnp.concatenate([gla_w_up, jnp.zeros((pad, GLA_KW), gla_w_up.dtype)], axis=0).astype(BF16)
    row2 = lambda a: a.reshape(1, -1)
    q, k, v, sgate = _attn_proj(x2, row2(norm_g), w_in, layer, attn_first, row2(sb_qn_g), row2(sb_kn_g))
    mixc, mixg = _conv_gla(
        x2, row2(norm_g), w_in, gla_first, w_lr, layer, conv_w, row2(conv_b), wup, row2(gla_b_up),
        row2(gla_on_g), batch=batch, seq=seq)

    r3 = lambda a: a.reshape(batch, seq, a.shape[-1])
    mixs = _stick_breaking(r3(q), r3(k), r3(v), r3(sgate))
    return _out_proj(x2, mixc, mixs.reshape(batch * seq, SB_W), mixg, w_out, layer)


def kernel(x, norm_g, w_in, conv_w, conv_b, sb_qn_g, sb_kn_g, gla_w_up, gla_b_up, gla_on_g, w_out):
    batch, seq, d = x.shape
    assert seq % ROW_TILE == 0 and seq % (SB_BLOCK * SB_SUBTILES) == 0
    assert seq % (GLA_CHUNK * GLA_STEP_CHUNKS) == 0
    attn_first = 4 * conv_w.shape[2]
    gla_first = attn_first + 4 * SB_W
    lr_first = gla_first + 2 * GLA_KW + 2 * GLA_W
    assert w_in.shape[2] - lr_first == GLA_LR
    w_in_b = w_in.astype(BF16)
    w_lr = jnp.pad(w_in[:, :, lr_first:], ((0, 0), (0, 0), (0, LANES - GLA_LR))).astype(BF16)
    h = x.reshape(batch * seq, d)
    for l in range(norm_g.shape[0]):
        h = _layer(h, batch, seq, l, w_in_b, w_lr, attn_first, gla_first, w_out, norm_g[l], conv_w[l],
                   conv_b[l], sb_qn_g[l], sb_kn_g[l], gla_w_up[l], gla_b_up[l], gla_on_g[l])
    return h.reshape(batch, seq, d)
```

```python
import functools

import numpy as np
import jax
import jax.numpy as jnp
from jax import lax
from jax.experimental import pallas as pl
from jax.experimental.pallas import tpu as pltpu

SB_HEADS = 8
SB_HD = 128
SB_W = SB_HEADS * SB_HD
GLA_HEADS = 4
GLA_DV = 128
GLA_DK = 64
GLA_W = GLA_HEADS * GLA_DV
GLA_KW = GLA_HEADS * GLA_DK
GLA_LR = 16
GLA_TAU = 16.0
CONV_K = 3
EPS = 1e-6

LANES = 128
SUBLANES = 8
VMEM_LIMIT_BYTES = 56 * 1024 * 1024

ROW_TILE = 512
COL_CHUNK = 256
GLA_CHUNK = 128
GLA_STEP_CHUNKS = 4
GLA_LEVELS = 7
GLA_MATMUL_LEVELS = 3
SB_BLOCK = 256
SB_SUBTILES = 32
SB_NEAR_ROWS = 160
MASKED = -1e30
EXHAUSTED = 1e30
F32_EXP2_UNDERFLOW = 150.0
LOG2_E = 1.4426950408889634

F32 = jnp.float32
BF16 = jnp.bfloat16


def _dot(a, b):
    return jnp.dot(a, b, preferred_element_type=F32)


def _dot_nt(a, b):
    return lax.dot_general(a, b, (((1,), (1,)), ((), ())), preferred_element_type=F32)


def _silu(a):
    return a / (1.0 + jnp.exp(-a))


def _normed_input(x_ref, ng_ref):
    xf = x_ref[...]
    ms = jnp.mean(xf * xf, axis=-1, keepdims=True)
    return (xf * lax.rsqrt(ms + EPS) * ng_ref[...]).astype(BF16)


def _head_rmsnorm(a, gain):
    outs = []
    for j in range(a.shape[1] // LANES):
        aj = a[:, j * LANES:(j + 1) * LANES]
        ms = jnp.mean(aj * aj, axis=-1, keepdims=True)
        outs.append(aj * lax.rsqrt(ms + EPS) * gain)
    return outs


def _attn_proj_kernel(x_ref, ng_ref, wqk_ref, wvg_ref, qg_ref, kg_ref, q_ref, k_ref, v_ref, g_ref):
    h = _normed_input(x_ref, ng_ref)
    q_scale = SB_HD ** -0.5 * LOG2_E
    n_chunks = SB_W // COL_CHUNK
    for c in range(n_chunks):
        cols = slice(c * COL_CHUNK, (c + 1) * COL_CHUNK)
        qa = _dot(h, wqk_ref[:, c * COL_CHUNK:(c + 1) * COL_CHUNK])
        for j, qn in enumerate(_head_rmsnorm(qa, qg_ref[...])):
            q_ref[:, c * COL_CHUNK + j * LANES:c * COL_CHUNK + (j + 1) * LANES] = (qn * q_scale).astype(BF16)
        ka = _dot(h, wqk_ref[:, SB_W + c * COL_CHUNK:SB_W + (c + 1) * COL_CHUNK])
        for j, kn in enumerate(_head_rmsnorm(ka, kg_ref[...])):
            k_ref[:, c * COL_CHUNK + j * LANES:c * COL_CHUNK + (j + 1) * LANES] = kn.astype(BF16)
        va = _dot(h, wvg_ref[:, c * COL_CHUNK:(c + 1) * COL_CHUNK])
        v_ref[:, cols] = va.astype(BF16)
        ga = _dot(h, wvg_ref[:, SB_W + c * COL_CHUNK:SB_W + (c + 1) * COL_CHUNK])
        g_ref[:, cols] = _silu(ga)


def _attn_proj(x2, norm_g, w_in, layer, first_col, qn_g, kn_g):
    m, d = x2.shape
    row = lambda i: (i, 0)
    const = lambda i: (0, 0)
    wblk = 2 * SB_W
    assert first_col % wblk == 0
    wspec = lambda n: pl.BlockSpec((None, d, wblk), lambda i: (layer, 0, first_col // wblk + n),
                                   pipeline_mode=pl.Buffered(1))
    return pl.pallas_call(
        _attn_proj_kernel,
        grid=(m // ROW_TILE,),
        in_specs=[
            pl.BlockSpec((ROW_TILE, d), row),
            pl.BlockSpec((1, d), const),
            wspec(0),
            wspec(1),
            pl.BlockSpec((1, SB_HD), const),
            pl.BlockSpec((1, SB_HD), const),
        ],
        out_specs=[pl.BlockSpec((ROW_TILE, SB_W), row)] * 4,
        out_shape=[
            jax.ShapeDtypeStruct((m, SB_W), BF16),
            jax.ShapeDtypeStruct((m, SB_W), BF16),
            jax.ShapeDtypeStruct((m, SB_W), BF16),
            jax.ShapeDtypeStruct((m, SB_W), F32),
        ],
        compiler_params=pltpu.CompilerParams(
            dimension_semantics=("parallel",), vmem_limit_bytes=VMEM_LIMIT_BYTES),
        name="attn_proj",
    )(x2, norm_g, w_in, w_in, qn_g, kn_g)


def _conv_gla_kernel(x_ref, ng_ref, wc_ref, wg_ref, wlr_ref, cw_ref, cb_ref, wup_ref, bup_ref,
                     ong_ref, sel_ref, msk_ref, hm_ref, bd_ref, mixc_ref, mixg_ref, ubuf_ref,
                     state_ref, *, conv_w):
    tm = x_ref.shape[0]
    halo = SUBLANES

    @pl.when(pl.program_id(1) == 0)
    def _():
        ubuf_ref[0:halo, :] = jnp.zeros((halo, conv_w), F32)
        state_ref[...] = jnp.zeros_like(state_ref)

    h = _normed_input(x_ref, ng_ref)
    proj = lambda w_ref, first, width: _dot(h, w_ref[:, first:first + width])
    lr = proj(wlr_ref, 0, LANES)
    pre = _dot(lr.astype(BF16), wup_ref[...]) + bup_ref[...]
    log_decay = (jnp.minimum(pre, 0.0) - jnp.log(1.0 + jnp.exp(-jnp.abs(pre)))) / GLA_TAU
    gl_q = proj(wg_ref, 0, GLA_KW)
    gl_k = proj(wg_ref, GLA_KW, GLA_KW)
    gl_v = proj(wg_ref, 2 * GLA_KW, GLA_W).astype(BF16)
    gate = _silu(proj(wg_ref, 2 * GLA_KW + GLA_W, GLA_W))
    e_alls = _gla_cumsums(log_decay, sel_ref)

    k_slice = x_ref.shape[1] // (GLA_LEVELS + 1)
    conv = [None] * 4

    def conv_piece(n, s):
        part = _dot(h[:, s * k_slice:(s + 1) * k_slice],
                    wc_ref[s * k_slice:(s + 1) * k_slice, n * conv_w:(n + 1) * conv_w])
        conv[n] = part if conv[n] is None else conv[n] + part

    fillers = [functools.partial(conv_piece, n, s) for n in range(4) for s in range(GLA_LEVELS + 1)]

    def conv_finish():
        cvh, cvb, cvc, cvg = conv
        u = cvc * cvh
        ubuf_ref[halo:halo + tm, :] = u
        um1 = ubuf_ref[halo - 1:halo - 1 + tm, :]
        um2 = ubuf_ref[halo - 2:halo - 2 + tm, :]
        y = cw_ref[0:1, :] * um2 + cw_ref[1:2, :] * um1 + cw_ref[2:3, :] * u + cb_ref[...]
        mixc_ref[...] = (cvb * y * _silu(cvg)).astype(BF16)
        ubuf_ref[0:halo, :] = u[tm - halo:tm, :]

    _gla_chunks(gl_q, gl_k, gl_v, e_alls, gate, ong_ref, msk_ref, hm_ref, bd_ref, state_ref, mixg_ref,
                fillers, conv_finish)


def _conv_gla(x2, norm_g, w_in, gla_first, w_lr, layer, conv_w, conv_b, wup, bup, on_g, *, batch, seq):
    m, d = x2.shape
    cw = conv_w.shape[1]
    assert ROW_TILE == GLA_CHUNK * GLA_STEP_CHUNKS
    n_tiles = seq // ROW_TILE
    row = lambda b, j: (b * n_tiles + j, 0)
    const = lambda b, j: (0, 0)
    const3 = lambda b, j: (0, 0, 0)
    wconst = lambda b, j: (layer, 0, 0)
    gla_cols = 2 * GLA_KW + 2 * GLA_W
    assert gla_first % gla_cols == 0
    sel, msk, hm, bd = _gla_tables()
    return pl.pallas_call(
        functools.partial(_conv_gla_kernel, conv_w=cw),
        grid=(batch, n_tiles),
        in_specs=[
            pl.BlockSpec((ROW_TILE, d), row),
            pl.BlockSpec((1, d), const),
            pl.BlockSpec((None, d, 4 * cw), wconst, pipeline_mode=pl.Buffered(1)),
            pl.BlockSpec((None, d, gla_cols), lambda b, j: (layer, 0, gla_first // gla_cols),
                         pipeline_mode=pl.Buffered(1)),
            pl.BlockSpec((None, d, LANES), wconst),
            pl.BlockSpec((CONV_K, cw), const),
            pl.BlockSpec((1, cw), const),
            pl.BlockSpec((LANES, GLA_KW), const),
            pl.BlockSpec((1, GLA_KW), const),
            pl.BlockSpec((1, GLA_DV), const),
            pl.BlockSpec(sel.shape, const),
            pl.BlockSpec(msk.shape, const3),
            pl.BlockSpec(hm.shape, const3),
            pl.BlockSpec(bd.shape, const),
        ],
        out_specs=[pl.BlockSpec((ROW_TILE, cw), row), pl.BlockSpec((ROW_TILE, GLA_W), row)],
        out_shape=[jax.ShapeDtypeStruct((m, cw), BF16), jax.ShapeDtypeStruct((m, GLA_W), BF16)],
        scratch_shapes=[pltpu.VMEM((ROW_TILE + SUBLANES, cw), F32), pltpu.VMEM((GLA_KW, GLA_W), F32)],
        compiler_params=pltpu.CompilerParams(
            dimension_semantics=("arbitrary", "arbitrary"), vmem_limit_bytes=VMEM_LIMIT_BYTES),
        name="conv_gla",
    )(x2, norm_g, w_in, w_in, w_lr, conv_w, conv_b, wup, bup, on_g, jnp.asarray(sel, BF16),
      jnp.asarray(msk, BF16), jnp.asarray(hm, BF16), jnp.asarray(bd))


def _gla_tables():
    c, nl = GLA_CHUNK, GLA_LEVELS
    t = np.arange(c)
    nm = GLA_MATMUL_LEVELS
    sel = np.zeros((nm + 1, c, c), np.float32)
    msk = np.zeros((nl + 1, c, c), np.float32)
    for l in range(nl):
        up = ((t >> l) & 1) == 1
        lo_start = (t >> l) << l
        nxt = ((t >> l) + 1) << l
        for r in range(c if l < nm else 0):
            if up[r]:
                sel[l, r, lo_start[r]:r + 1] = 1.0
            else:
                sel[l, r, r + 1:nxt[r]] = 1.0
        same = (t[:, None] >> (l + 1)) == (t[None, :] >> (l + 1))
        msk[l] = (up[:, None] & ~up[None, :] & same).astype(np.float32)
    sel[nm] = (t[None, :] <= t[:, None]).astype(np.float32)
    msk[nl] = np.eye(c, dtype=np.float32)
    sel = sel.reshape((nm + 1) * c, c)
    msk = np.tile(msk, (1, 1, GLA_HEADS))
    head_of_k = np.arange(GLA_KW) // GLA_DK
    hmask = (head_of_k[None, :] == np.arange(GLA_HEADS)[:, None]).astype(np.float32)
    bd = (head_of_k[:, None] == (np.arange(GLA_W) // GLA_DV)[None, :]).astype(np.float32)
    return sel, msk, hmask.reshape(GLA_HEADS, 1, GLA_KW), bd


def _gla_cumsums(g_all, sel_ref):
    c = GLA_CHUNK
    sel = sel_ref[...]
    e_alls = []
    for ci in range(GLA_STEP_CHUNKS):
        g = g_all[ci * c:(ci + 1) * c, :]
        g1 = g.astype(BF16)
        g2 = (g - g1.astype(F32)).astype(BF16)
        e_alls.append(_dot(sel, g1) + _dot(sel, g2))
    return e_alls


def _gla_chunks(q_all, k_all, v_all, e_alls, gate_all, ong_ref, msk_ref, hm_ref, bd_ref,
                state_ref, o_ref, fillers, after_levels):
    fillers = list(fillers)
    assert len(fillers) == GLA_STEP_CHUNKS * (GLA_LEVELS + 1)
    c, nl, nm = GLA_CHUNK, GLA_LEVELS, GLA_MATMUL_LEVELS

    def stacked_k(kl):
        kb = kl.astype(BF16)
        return jnp.concatenate([kb * hm_ref[hh] for hh in range(GLA_HEADS)], axis=0)

    chunks = range(GLA_STEP_CHUNKS)
    rows = [slice(ci * c, (ci + 1) * c) for ci in chunks]

    def level_exponent(e_all, cum, l):
        if l < nm:
            return e_all[l * c:(l + 1) * c, :]
        slabs = []
        for r0 in range(0, c, SUBLANES):
            slab = cum[r0:r0 + SUBLANES, :]
            if (r0 >> l) & 1:
                first = (r0 >> l) << l
                slabs.append(slab - cum[first - 1:first, :])
            else:
                last_row = (((r0 >> l) + 1) << l) - 1
                slabs.append(cum[last_row:last_row + 1, :] - slab)
        return jnp.concatenate(slabs, axis=0)

    atts, q_decs, updates, decays = [], [], [], []
    for ci in chunks:
        e_all = e_alls[ci]
        cum = e_all[nm * c:(nm + 1) * c, :]
        q = q_all[rows[ci], :] * (GLA_DK ** -0.5)
        k = k_all[rows[ci], :]
        v = v_all[rows[ci], :]
        att = msk_ref[nl] * _dot_nt(q.astype(BF16), stacked_k(k)).astype(BF16)
        fillers.pop(0)()
        for l in range(nl):
            e = jnp.exp(level_exponent(e_all, cum, l))
            att = att + msk_ref[l] * _dot_nt((q * e).astype(BF16), stacked_k(k * e)).astype(BF16)
            fillers.pop(0)()
        atts.append(att)
        q_decs.append((q * jnp.exp(cum)).astype(BF16))
        last = cum[c - 1:c, :]
        k_dec_t = (k * jnp.exp(last - cum)).T
        decay_t = jnp.broadcast_to(jnp.exp(last), (c, GLA_KW)).T
        decays.append(jnp.concatenate([decay_t] * (GLA_W // c), axis=1))
        updates.append(bd_ref[...] * _dot(k_dec_t.astype(BF16), v))

    after_levels()
    state = state_ref[...]
    for ci in chunks:
        v = v_all[rows[ci], :]
        o = _dot(q_decs[ci], state.astype(BF16))
        intra = [_dot(atts[ci][:, hh * c:(hh + 1) * c], v[:, hh * GLA_DV:(hh + 1) * GLA_DV])
                 for hh in range(GLA_HEADS)]
        o = o + jnp.concatenate(intra, axis=1)
        state = state * decays[ci] + updates[ci]
        normed = _head_rmsnorm(o, ong_ref[...])
        o_ref[rows[ci], :] = (jnp.concatenate(normed, axis=1) * gate_all[rows[ci], :]).astype(BF16)
    state_ref[...] = state


def _sb_kernel(q_ref, k_ref, v_ref, gate_ref, tri_ref, o_ref):
    blk, nsub = SB_BLOCK, SB_SUBTILES
    i = pl.program_id(2)
    tri = tri_ref[...]
    diag = [i * nsub + a for a in range(nsub)]
    strictly_causal = (lax.broadcasted_iota(jnp.int32, (blk, blk), 1)
                       < lax.broadcasted_iota(jnp.int32, (blk, blk), 0))

    def key_rows(ref, kb):
        return ref[0, pl.ds(pl.multiple_of(kb * blk, blk), blk), :]

    def sub_rows(a, part):
        return slice(a * blk + part.start, a * blk + part.stop)

    def visit(blocks, part, carries, accs, on_diagonal):
        n_rows = part.stop - part.start
        carries, accs = list(carries), list(accs)
        zs, sps, suffixes, ws = ([None] * nsub for _ in range(4))

        def scores(a):
            zs[a] = _dot_nt(q_ref[0, sub_rows(a, part), :], key_rows(k_ref, blocks[a]))

        def softplus2(a):
            z = zs[a]
            sp = jnp.maximum(z, 0.0) + jnp.log2(1.0 + jnp.exp2(jnp.minimum(z, -z)))
            if on_diagonal:
                visible = strictly_causal[part, :]
                sp = jnp.where(visible, sp, 0.0)
                zs[a] = jnp.where(visible, z, MASKED)
            sps[a] = sp.astype(BF16)

        def suffix_sum(a):
            suffixes[a] = _dot(sps[a], tri)

        def weights(a):
            d = jnp.minimum(zs[a] - suffixes[a], 0.0)
            carry = jnp.concatenate([carries[a]] * (blk // LANES), axis=1)
            ws[a] = jnp.exp2(d - carry).astype(BF16)
            carries[a] = carries[a] + jnp.broadcast_to(suffixes[a][:, 0:1], (n_rows, LANES))

        def accumulate(a):
            accs[a] = accs[a] + _dot(ws[a], key_rows(v_ref, blocks[a]))

        stages = (scores, softplus2, suffix_sum, weights, accumulate)
        for t in range(nsub + len(stages) - 1):
            for a in range(nsub):
                if 0 <= t - a < len(stages):
                    stages[t - a](a)
        return carries, accs

    def unfinished(carries, next_step):
        lowest = None
        for a in range(nsub):
            c = jnp.min(carries[a].reshape(-1, SUBLANES, LANES), axis=0)
            c = jnp.where(diag[a] - next_step < 0, EXHAUSTED, c)
            lowest = c if lowest is None else jnp.minimum(lowest, c)
        return jnp.min(lowest) < F32_EXP2_UNDERFLOW

    def walk_back(part, carries, accs):
        def body(state):
            _, step, carries, accs = state
            carries = [jnp.where(diag[a] - step < 0, EXHAUSTED, carries[a]) for a in range(nsub)]
            blocks = [jnp.maximum(diag[a] - step, 0) for a in range(nsub)]
            carries, accs = visit(blocks, part, carries, accs, on_diagonal=False)
            return unfinished(carries, step + 1), step + 1, tuple(carries), tuple(accs)

        state = (unfinished(carries, 1), jnp.int32(1), tuple(carries), tuple(accs))
        return lax.while_loop(lambda state: state[0], body, state)[3]

    carries, accs = visit(diag, slice(0, blk), [jnp.zeros((blk, LANES), F32)] * nsub,
                          [jnp.zeros((blk, SB_HD), F32)] * nsub, on_diagonal=True)
    for part in (slice(0, SB_NEAR_ROWS), slice(SB_NEAR_ROWS, blk)):
        part_accs = walk_back(part, [c[part] for c in carries], [x[part] for x in accs])
        for a in range(nsub):
            rows_a = sub_rows(a, part)
            o_ref[0, rows_a, :] = (part_accs[a] * gate_ref[0, rows_a, :]).astype(BF16)


def _stick_breaking(q, k, v, gate):
    b, s, _ = q.shape
    blk = SB_BLOCK
    qblk = SB_BLOCK * SB_SUBTILES
    tri = (np.arange(blk)[:, None] >= np.arange(blk)[None, :]).astype(np.float32)
    qspec = pl.BlockSpec((1, qblk, SB_HD), lambda bi, h, i: (bi, i, h))
    kvspec = pl.BlockSpec((1, s, SB_HD), lambda bi, h, i: (bi, 0, h))
    return pl.pallas_call(
        _sb_kernel,
        grid=(b, SB_HEADS, s // qblk),
        in_specs=[qspec, kvspec, kvspec, qspec, pl.BlockSpec((blk, blk), lambda bi, h, i: (0, 0))],
        out_specs=qspec,
        out_shape=jax.ShapeDtypeStruct((b, s, SB_W), BF16),
        compiler_params=pltpu.CompilerParams(
            dimension_semantics=("parallel", "parallel", "arbitrary"),
            vmem_limit_bytes=VMEM_LIMIT_BYTES),
        name="stick_breaking",
    )(q, k, v, gate, jnp.asarray(tri, BF16))


def _out_proj_kernel(x_ref, mc_ref, ms_ref, mg_ref, w_ref, o_ref):
    cw = mc_ref.shape[1]
    y = _dot(mc_ref[...], w_ref[0:cw, :].astype(BF16))
    y = y + _dot(ms_ref[...], w_ref[cw:cw + SB_W, :].astype(BF16))
    y = y + _dot(mg_ref[...], w_ref[cw + SB_W:cw + SB_W + GLA_W, :].astype(BF16))
    o_ref[...] = x_ref[...] + y


def _out_proj(x2, mixc, mixs, mixg, w_out, layer):
    m, d = x2.shape
    row = lambda i: (i, 0)
    return pl.pallas_call(
        _out_proj_kernel,
        grid=(m // ROW_TILE,),
        in_specs=[
            pl.BlockSpec((ROW_TILE, d), row),
            pl.BlockSpec((ROW_TILE, mixc.shape[1]), row),
            pl.BlockSpec((ROW_TILE, SB_W), row),
            pl.BlockSpec((ROW_TILE, GLA_W), row),
            pl.BlockSpec((None,) + w_out.shape[1:], lambda i: (layer, 0, 0), pipeline_mode=pl.Buffered(1)),
        ],
        out_specs=pl.BlockSpec((ROW_TILE, d), row),
        out_shape=jax.ShapeDtypeStruct((m, d), F32),
        compiler_params=pltpu.CompilerParams(
            dimension_semantics=("parallel",), vmem_limit_bytes=VMEM_LIMIT_BYTES),
        name="out_proj",
    )(x2, mixc, mixs, mixg, w_out)


def _layer(x2, batch, seq, layer, w_in, w_lr, attn_first, gla_first, w_out, norm_g, conv_w, conv_b,
           sb_qn_g, sb_kn_g, gla_w_up, gla_b_up, gla_on_g):
    pad = LANES - GLA_LR
    wup = jnp.concatenate([gla_w_up, jnp.zeros((pad, GLA_KW), gla_w_up.dtype)], axis=0).astype(BF16)
    row2 = lambda a: a.reshape(1, -1)
    q, k, v, sgate = _attn_proj(x2, row2(norm_g), w_in, layer, attn_first, row2(sb_qn_g), row2(sb_kn_g))
    mixc, mixg = _conv_gla(
        x2, row2(norm_g), w_in, gla_first, w_lr, layer, conv_w, row2(conv_b), wup, row2(gla_b_up),
        row2(gla_on_g), batch=batch, seq=seq)

    r3 = lambda a: a.reshape(batch, seq, a.shape[-1])
    mixs = _stick_breaking(r3(q), r3(k), r3(v), r3(sgate))
    return _out_proj(x2, mixc, mixs.reshape(batch * seq, SB_W), mixg, w_out, layer)


def kernel(x, norm_g, w_in, conv_w, conv_b, sb_qn_g, sb_kn_g, gla_w_up, gla_b_up, gla_on_g, w_out):
    batch, seq, d = x.shape
    assert seq % ROW_TILE == 0 and seq % (SB_BLOCK * SB_SUBTILES) == 0
    assert seq % (GLA_CHUNK * GLA_STEP_CHUNKS) == 0
    attn_first = 4 * conv_w.shape[2]
    gla_first = attn_first + 4 * SB_W
    lr_first = gla_first + 2 * GLA_KW + 2 * GLA_W
    assert w_in.shape[2] - lr_first == GLA_LR
    w_in_b = w_in.astype(BF16)
    w_lr = jnp.pad(w_in[:, :, lr_first:], ((0, 0), (0, 0), (0, LANES - GLA_LR))).astype(BF16)
    h = x.reshape(batch * seq, d)
    for l in range(norm_g.shape[0]):
        h = _layer(h, batch, seq, l, w_in_b, w_lr, attn_first, gla_first, w_out, norm_g[l], conv_w[l],
                   conv_b[l], sb_qn_g[l], sb_kn_g[l], gla_w_up[l], gla_b_up[l], gla_on_g[l])
    return h.reshape(batch, seq, d)
```
